```python
import math
import jax, jax.numpy as jnp
from jax import lax
import numpy as np

D_MODEL = 1024
BATCH = 4
SEQ = 8192
DEPTH = 1

HEAD_DIM = 64
DA_HEADS = 4
DA_QK = DA_HEADS * 2 * HEAD_DIM
DA_V = DA_HEADS * 2 * HEAD_DIM
NA_HEADS = 8
NA_W = NA_HEADS * HEAD_DIM
GRID_W = 64
NA_ROWS = 8
NA_COLS = 16
Q_BLOCK = 128
D_FF = int(math.ceil(8 * D_MODEL / 3 / 256) * 256)
IN_COLS = 2 * DA_QK + DA_V + 3 * NA_W + 2 * D_MODEL
EPS = 1e-6
NEG = -1e30

kernel_name = "hybrid_diffattn_natten_gated_encoder"


def rmsnorm(x, w):
    xf = x.astype(jnp.float32)
    y = xf * lax.rsqrt(jnp.mean(xf * xf, axis=-1, keepdims=True) + EPS)
    return (y * w.astype(jnp.float32)).astype(x.dtype)


def alibi_slopes(n):
    return jnp.asarray([2.0 ** (-8.0 * (i + 1) / n) for i in range(n)], dtype=jnp.float32)


def diff_attention(q, k, v, lam, subln_w, lambda_init):
    B, S, H, _, d = q.shape
    nb = S // Q_BLOCK
    scale = d ** -0.5
    slopes = alibi_slopes(H)
    kpos = jnp.arange(S)
    qb = jnp.moveaxis(q.reshape(B, nb, Q_BLOCK, H, 2, d), 1, 0)

    def block(args):
        qi, i = args
        s = jnp.einsum('bqhcd,bkhcd->bhcqk', qi, k).astype(jnp.float32) * scale
        qpos = i * Q_BLOCK + jnp.arange(Q_BLOCK)
        dist = jnp.abs(qpos[:, None] - kpos[None, :]).astype(jnp.float32)
        s = s - slopes[None, :, None, None, None] * dist[None, None, None]
        p = jax.nn.softmax(s, axis=-1)
        a = p[:, :, 0] - lam * p[:, :, 1]
        return jnp.einsum('bhqk,bkhe->bqhe', a.astype(v.dtype), v)

    o = lax.map(block, (qb, jnp.arange(nb)))
    o = jnp.moveaxis(o, 0, 1).reshape(B, S, H, 2 * d)
    o = rmsnorm(o, subln_w) * (1.0 - lambda_init)
    return o.reshape(B, S, H * 2 * d)


def neighbourhood_attention_2d(q, k, v, rpb):
    B, S, H, d = q.shape
    rows = S // GRID_W
    kr = min(NA_ROWS, rows)
    kc = NA_COLS
    scale = d ** -0.5
    qg = q.reshape(B, rows, GRID_W, H, d)
    kg = k.reshape(B, rows, GRID_W, H, d)
    vg = v.reshape(B, rows, GRID_W, H, d)
    col = jnp.arange(GRID_W)
    col_start = jnp.clip(col - kc // 2, 0, GRID_W - kc)
    col_in = (col[None, :] >= col_start[:, None]) & (col[None, :] < col_start[:, None] + kc)
    col_off = jnp.clip(col[None, :] - col[:, None] + NA_COLS - 1, 0, 2 * NA_COLS - 2)
    rpb_cols = rpb[:, :, col_off]

    def row_step(args):
        qr, r = args
        rs = jnp.clip(r - kr // 2, 0, rows - kr)
        kb = lax.dynamic_slice_in_dim(kg, rs, kr, axis=1)
        vb = lax.dynamic_slice_in_dim(vg, rs, kr, axis=1)
        s = jnp.einsum('bqhd,bikhd->bhqik', qr, kb).astype(jnp.float32) * scale
        row_off = rs + jnp.arange(kr) - r + NA_ROWS - 1
        bias = jnp.transpose(rpb_cols[:, row_off], (0, 2, 1, 3))
        s = s + bias[None].astype(jnp.float32)
        s = jnp.where(col_in[None, None, :, None, :], s, NEG)
        p = jax.nn.softmax(s, axis=(-2, -1))
        return jnp.einsum('bhqik,bikhd->bqhd', p.astype(v.dtype), vb)

    o = lax.map(row_step, (jnp.moveaxis(qg, 1, 0), jnp.arange(rows)))
    return jnp.moveaxis(o, 0, 1).reshape(B, S, H * d)


def setup_inputs(seed: int = 0) -> dict:
    key = jax.random.key(seed)
    ks = jax.random.split(key, 24)
    f32 = jnp.float32

    def nrm(k, shape, scale):
        return jax.random.normal(k, shape, f32) * scale

    def gain(k):
        return 1.0 + nrm(k, (DEPTH, D_MODEL), 0.02)

    return {
        "x": nrm(ks[0], (BATCH, SEQ, D_MODEL), 1.0),
        "pre_mix_w": gain(ks[1]),
        "w_in": nrm(ks[2], (DEPTH, D_MODEL, IN_COLS), D_MODEL ** -0.5),
        "b_gate": nrm(ks[3], (DEPTH, 2 * D_MODEL), 0.1),
        "lambda_q1": nrm(ks[4], (DEPTH, HEAD_DIM), 0.1),
        "lambda_k1": nrm(ks[5], (DEPTH, HEAD_DIM), 0.1),
        "lambda_q2": nrm(ks[6], (DEPTH, HEAD_DIM), 0.1),
        "lambda_k2": nrm(ks[7], (DEPTH, HEAD_DIM), 0.1),
        "subln_w": 1.0 + nrm(ks[8], (DEPTH, 2 * HEAD_DIM), 0.02),
        "rpb": nrm(ks[9], (DEPTH, NA_HEADS, 2 * NA_ROWS - 1, 2 * NA_COLS - 1), 0.1),
        "w_branch_a": nrm(ks[10], (DEPTH, DA_V, D_MODEL), DA_V ** -0.5),
        "w_branch_b": nrm(ks[11], (DEPTH, NA_W, D_MODEL), NA_W ** -0.5),
        "w_out": nrm(ks[12], (DEPTH, D_MODEL, D_MODEL), D_MODEL ** -0.5),
        "post_mix_w": gain(ks[13]),
        "pre_ffn_w": gain(ks[14]),
        "w_gate": nrm(ks[15], (DEPTH, D_MODEL, D_FF), D_MODEL ** -0.5),
        "w_up": nrm(ks[16], (DEPTH, D_MODEL, D_FF), D_MODEL ** -0.5),
        "w_down": nrm(ks[17], (DEPTH, D_FF, D_MODEL), D_FF ** -0.5),
        "post_ffn_w": gain(ks[18]),
    }


def reference(x, pre_mix_w, w_in, b_gate, lambda_q1, lambda_k1, lambda_q2, lambda_k2,
              subln_w, rpb, w_branch_a, w_branch_b, w_out, post_mix_w, pre_ffn_w,
              w_gate, w_up, w_down, post_ffn_w):
    B, S, _ = x.shape
    for l in range(DEPTH):
        lambda_init = 0.8 - 0.6 * math.exp(-0.3 * l)
        h = rmsnorm(x, pre_mix_w[l])
        proj = h @ w_in[l]
        o0 = 0
        q_a = proj[..., o0:o0 + DA_QK].reshape(B, S, DA_HEADS, 2, HEAD_DIM); o0 += DA_QK
        k_a = proj[..., o0:o0 + DA_QK].reshape(B, S, DA_HEADS, 2, HEAD_DIM); o0 += DA_QK
        v_a = proj[..., o0:o0 + DA_V].reshape(B, S, DA_HEADS, 2 * HEAD_DIM); o0 += DA_V
        q_b = proj[..., o0:o0 + NA_W].reshape(B, S, NA_HEADS, HEAD_DIM); o0 += NA_W
        k_b = proj[..., o0:o0 + NA_W].reshape(B, S, NA_HEADS, HEAD_DIM); o0 += NA_W
        v_b = proj[..., o0:o0 + NA_W].reshape(B, S, NA_HEADS, HEAD_DIM); o0 += NA_W
        gates = jax.nn.sigmoid(proj[..., o0:] + b_gate[l])
        g_a, g_b = gates[..., :D_MODEL], gates[..., D_MODEL:]

        lq1 = lambda_q1[l].astype(jnp.float32); lk1 = lambda_k1[l].astype(jnp.float32)
        lq2 = lambda_q2[l].astype(jnp.float32); lk2 = lambda_k2[l].astype(jnp.float32)
        lam = jnp.exp(jnp.sum(lq1 * lk1)) - jnp.exp(jnp.sum(lq2 * lk2)) + lambda_init

        y_a = diff_attention(q_a, k_a, v_a, lam, subln_w[l], lambda_init)
        y_b = neighbourhood_attention_2d(q_b, k_b, v_b, rpb[l])
        merged = g_a * (y_a @ w_branch_a[l]) + g_b * (y_b @ w_branch_b[l])
        x = x + rmsnorm(merged @ w_out[l], post_mix_w[l])
        h = rmsnorm(x, pre_ffn_w[l])
        f = (jax.nn.silu(h @ w_gate[l]) * (h @ w_up[l])) @ w_down[l]
        x = x + rmsnorm(f, post_ffn_w[l])
    return x
```

```python
import functools
import math

import jax
import jax.numpy as jnp
from jax import lax
from jax.experimental import pallas as pl
from jax.experimental.pallas import tpu as pltpu

D_MODEL = 1024
HEAD_DIM = 64
DA_HEADS = 4
NA_HEADS = 8
NA_PAIRS = NA_HEADS // 2
GRID_W = 64
NA_ROWS = 8
NA_COLS = 16
D_FF = 2816
IN_COLS = 5120
EPS = 1e-6
NEG = -1e30
LANES = 128
N_COLBLK = IN_COLS // LANES
CB_QA, CB_KA, CB_VA = 0, 4, 8
CB_QB, CB_KB, CB_VB = 12, 16, 20
CB_GA, CB_GB = 24, 32
LAMBDA_INIT = 0.8 - 0.6 * math.exp(-0.3 * 0)

VMEM_LIMIT = 56 * 1024 * 1024


def _rms(xf, w):
    return xf * lax.rsqrt(jnp.mean(xf * xf, axis=-1, keepdims=True) + EPS) * w


IN_TM = 512
IN_TN = 1024
IN_NBLK = IN_TN // LANES
GATE_J0 = (CB_GA * LANES) // IN_TN


def _in_proj_kernel(x_ref, nw_ref, w_ref, b_ref, o_ref):
    j = pl.program_id(0)
    h = _rms(x_ref[...], nw_ref[...]).astype(jnp.bfloat16)
    acc = jnp.dot(h, w_ref[...], preferred_element_type=jnp.float32)

    def store(val):
        for c in range(IN_NBLK):
            o_ref[c] = val[:, c * LANES:(c + 1) * LANES].astype(o_ref.dtype)

    @pl.when(j < GATE_J0)
    def _():
        store(acc)

    @pl.when(j >= GATE_J0)
    def _():
        store(jax.nn.sigmoid(acc + b_ref[...]))


def _in_proj(x2, nw, w_bf, b_full):
    m = x2.shape[0]
    return pl.pallas_call(
        _in_proj_kernel,
        grid=(IN_COLS // IN_TN, m // IN_TM),
        in_specs=[
            pl.BlockSpec((IN_TM, D_MODEL), lambda j, i: (i, 0)),
            pl.BlockSpec((1, D_MODEL), lambda j, i: (0, 0)),
            pl.BlockSpec((D_MODEL, IN_TN), lambda j, i: (0, j)),
            pl.BlockSpec((1, IN_TN), lambda j, i: (0, j)),
        ],
        out_specs=pl.BlockSpec((IN_NBLK, IN_TM, LANES), lambda j, i: (j, i, 0)),
        out_shape=jax.ShapeDtypeStruct((N_COLBLK, m, LANES), jnp.bfloat16),
        compiler_params=pltpu.CompilerParams(
            dimension_semantics=("arbitrary", "arbitrary"),
            vmem_limit_bytes=VMEM_LIMIT),
        name="in_proj",
    )(x2, nw, w_bf, b_full)


DA_TQ = 512
DA_TK = 512


def _diff_attn_kernel(slopes_ref, q_ref, k_ref, v_ref, lam_ref, sw_ref, o_ref,
                      m1, l1, a1, m2, l2, a2):
    h = pl.program_id(1)
    qi = pl.program_id(2)
    ki = pl.program_id(3)
    nk = pl.num_programs(3)

    @pl.when(ki == 0)
    def _():
        m1[...] = jnp.full_like(m1, -jnp.inf)
        m2[...] = jnp.full_like(m2, -jnp.inf)
        l1[...] = jnp.zeros_like(l1)
        l2[...] = jnp.zeros_like(l2)
        a1[...] = jnp.zeros_like(a1)
        a2[...] = jnp.zeros_like(a2)

    q = q_ref[...] * jnp.bfloat16(HEAD_DIM ** -0.5)
    k = k_ref[...]
    v = v_ref[...]
    lane = lax.broadcasted_iota(jnp.int32, q.shape, 1)
    zero = jnp.zeros_like(q)
    q1 = jnp.where(lane < HEAD_DIM, q, zero)
    q2 = jnp.where(lane >= HEAD_DIM, q, zero)

    qpos = qi * DA_TQ + lax.broadcasted_iota(jnp.int32, (DA_TQ, DA_TK), 0)
    kpos = ki * DA_TK + lax.broadcasted_iota(jnp.int32, (DA_TQ, DA_TK), 1)
    bias = jnp.abs(qpos - kpos).astype(jnp.float32) * (-slopes_ref[h])

    dn = (((1,), (1,)), ((), ()))

    def update(qm, m_ref, l_ref, a_ref):
        s = lax.dot_general(qm, k, dn, preferred_element_type=jnp.float32) + bias
        m_old = m_ref[...]
        m_new = jnp.maximum(m_old, jnp.max(s, axis=-1, keepdims=True))
        alpha = jnp.exp(m_old - m_new)
        p = jnp.exp(s - m_new)
        l_ref[...] = alpha * l_ref[...] + jnp.sum(p, axis=-1, keepdims=True)
        a_ref[...] = alpha * a_ref[...] + jnp.dot(
            p.astype(jnp.bfloat16), v, preferred_element_type=jnp.float32)
        m_ref[...] = m_new

    update(q1, m1, l1, a1)
    update(q2, m2, l2, a2)

    @pl.when(ki == nk - 1)
    def _():
        lv = lam_ref[...]
        lam = (jnp.exp(jnp.sum(lv[0:1] * lv[1:2], axis=-1, keepdims=True))
               - jnp.exp(jnp.sum(lv[2:3] * lv[3:4], axis=-1, keepdims=True))
               + LAMBDA_INIT)
        o = a1[...] / l1[...] - lam * (a2[...] / l2[...])
        o = _rms(o, sw_ref[...]) * (1.0 - LAMBDA_INIT)
        o_ref[...] = o.astype(o_ref.dtype)


def _diff_attn(proj, slopes, lam_vecs, subln_w, batch, seq):
    nq = seq // DA_TQ
    nk = seq // DA_TK
    m = batch * seq
    grid_spec = pltpu.PrefetchScalarGridSpec(
        num_scalar_prefetch=1,
        grid=(batch, DA_HEADS, nq, nk),
        in_specs=[
            pl.BlockSpec((None, DA_TQ, LANES), lambda b, h, qi, ki, s: (CB_QA + h, b * nq + qi, 0)),
            pl.BlockSpec((None, DA_TK, LANES), lambda b, h, qi, ki, s: (CB_KA + h, b * nk + ki, 0)),
            pl.BlockSpec((None, DA_TK, LANES), lambda b, h, qi, ki, s: (CB_VA + h, b * nk + ki, 0)),
            pl.BlockSpec((4, HEAD_DIM), lambda b, h, qi, ki, s: (0, 0)),
            pl.BlockSpec((1, 2 * HEAD_DIM), lambda b, h, qi, ki, s: (0, 0)),
        ],
        out_specs=pl.BlockSpec((None, DA_TQ, LANES), lambda b, h, qi, ki, s: (h, b * nq + qi, 0)),
        scratch_shapes=[
            pltpu.VMEM((DA_TQ, 1), jnp.float32),
            pltpu.VMEM((DA_TQ, 1), jnp.float32),
            pltpu.VMEM((DA_TQ, LANES), jnp.float32),
            pltpu.VMEM((DA_TQ, 1), jnp.float32),
            pltpu.VMEM((DA_TQ, 1), jnp.float32),
            pltpu.VMEM((DA_TQ, LANES), jnp.float32),
        ],
    )
    return pl.pallas_call(
        _diff_attn_kernel,
        grid_spec=grid_spec,
        out_shape=jax.ShapeDtypeStruct((DA_HEADS, m, LANES), jnp.bfloat16),
        compiler_params=pltpu.CompilerParams(
            dimension_semantics=("arbitrary", "arbitrary", "arbitrary", "arbitrary"),
            vmem_limit_bytes=VMEM_LIMIT),
        name="diff_attn",
    )(slopes, proj, proj, proj, lam_vecs, subln_w)


NA_GROUP = 8
NA_TOK = NA_GROUP * GRID_W
NA_WIN = NA_ROWS * GRID_W
NA_BIAS_ROWS = (2 * NA_ROWS - 1) * GRID_W


def _nbr_attn_kernel(q_ref, k_ref, v_ref, bias_ref, o_ref, *, rows):
    g = pl.program_id(2)
    lane_q = lax.broadcasted_iota(jnp.int32, (GRID_W, LANES), 1)
    dn_last = (((1,), (1,)), ((), ()))
    dn_first = (((0,), (0,)), ((), ()))
    for rr in range(NA_GROUP):
        r = g * NA_GROUP + rr
        rs = jnp.clip(r - NA_ROWS // 2, 0, rows - NA_ROWS)
        k0 = pl.multiple_of(rs * GRID_W, GRID_W)
        b0 = pl.multiple_of((NA_ROWS - 1 - (r - rs)) * GRID_W, GRID_W)
        kwin = k_ref[pl.ds(k0, NA_WIN), :]
        vwin = v_ref[pl.ds(k0, NA_WIN), :]
        bias = bias_ref[pl.ds(b0, NA_WIN), :]
        q = q_ref[rr * GRID_W:(rr + 1) * GRID_W, :] * jnp.bfloat16(HEAD_DIM ** -0.5)
        zero = jnp.zeros_like(q)
        qbd = jnp.concatenate([jnp.where(lane_q < HEAD_DIM, q, zero),
                               jnp.where(lane_q >= HEAD_DIM, q, zero)], axis=0)
        st = lax.dot_general(kwin, qbd, dn_last, preferred_element_type=jnp.float32) + bias
        mx = jnp.max(st, axis=0, keepdims=True)
        p = jnp.exp(st - mx)
        den = jnp.sum(p, axis=0, keepdims=True)
        p = p * (1.0 / den)
        full = lax.dot_general(p.astype(jnp.bfloat16), vwin, dn_first,
                               preferred_element_type=jnp.float32)
        o = jnp.where(lane_q < HEAD_DIM, full[:GRID_W], full[GRID_W:])
        o_ref[rr * GRID_W:(rr + 1) * GRID_W, :] = o.astype(o_ref.dtype)


def _nbr_attn(proj, bias_ext, batch, seq):
    rows = seq // GRID_W
    ng = seq // NA_TOK
    m = batch * seq
    return pl.pallas_call(
        functools.partial(_nbr_attn_kernel, rows=rows),
        grid=(batch, NA_PAIRS, ng),
        in_specs=[
            pl.BlockSpec((None, NA_TOK, LANES), lambda b, hp, g: (CB_QB + hp, b * ng + g, 0)),
            pl.BlockSpec((None, seq, LANES), lambda b, hp, g: (CB_KB + hp, b, 0)),
            pl.BlockSpec((None, seq, LANES), lambda b, hp, g: (CB_VB + hp, b, 0)),
            pl.BlockSpec((None, NA_BIAS_ROWS, LANES), lambda b, hp, g: (hp, 0, 0)),
        ],
        out_specs=pl.BlockSpec((None, NA_TOK, LANES), lambda b, hp, g: (hp, b * ng + g, 0)),
        out_shape=jax.ShapeDtypeStruct((NA_PAIRS, m, LANES), jnp.bfloat16),
        compiler_params=pltpu.CompilerParams(
            dimension_semantics=("arbitrary", "arbitrary", "arbitrary"),
            vmem_limit_bytes=VMEM_LIMIT),
        name="nbr_attn",
    )(proj, proj, proj, bias_ext)


def _nbr_bias_table(rpb_l):
    col = jnp.arange(GRID_W)
    col_start = jnp.clip(col - NA_COLS // 2, 0, GRID_W - NA_COLS)
    col_in = (col[None, :] >= col_start[:, None]) & (col[None, :] < col_start[:, None] + NA_COLS)
    col_off = jnp.clip(col[None, :] - col[:, None] + NA_COLS - 1, 0, 2 * NA_COLS - 2)
    t = rpb_l.astype(jnp.float32)[:, :, col_off]
    t = jnp.where(col_in[None, None], t, NEG)
    t = t.reshape(NA_PAIRS, 2, 2 * NA_ROWS - 1, GRID_W, GRID_W)
    t = jnp.transpose(t, (0, 2, 4, 1, 3))
    return t.reshape(NA_PAIRS, NA_BIAS_ROWS, LANES)


MIX_TM = 512


def _cat_blocks(ref):
    return jnp.concatenate([ref[c] for c in range(ref.shape[0])], axis=-1)


def _mix_out_kernel(x_ref, ya_ref, yb_ref, ga_ref, gb_ref, wa_ref, wb_ref, wo_ref, nw_ref, o_ref):
    ya = _cat_blocks(ya_ref)
    yb = _cat_blocks(yb_ref)
    ga = _cat_blocks(ga_ref).astype(jnp.float32)
    gb = _cat_blocks(gb_ref).astype(jnp.float32)
    merged = (ga * jnp.dot(ya, wa_ref[...], preferred_element_type=jnp.float32)
              + gb * jnp.dot(yb, wb_ref[...], preferred_element_type=jnp.float32))
    t = jnp.dot(merged.astype(jnp.bfloat16), wo_ref[...], preferred_element_type=jnp.float32)
    o_ref[...] = x_ref[...] + _rms(t, nw_ref[...])


def _mix_out(x2, ya, yb, proj, wa, wb, wo, nw):
    m = x2.shape[0]
    const = lambda i: (0, 0)
    return pl.pallas_call(
        _mix_out_kernel,
        grid=(m // MIX_TM,),
        in_specs=[
            pl.BlockSpec((MIX_TM, D_MODEL), lambda i: (i, 0)),
            pl.BlockSpec((DA_HEADS, MIX_TM, LANES), lambda i: (0, i, 0)),
            pl.BlockSpec((NA_PAIRS, MIX_TM, LANES), lambda i: (0, i, 0)),
            pl.BlockSpec((8, MIX_TM, LANES), lambda i: (CB_GA // 8, i, 0)),
            pl.BlockSpec((8, MIX_TM, LANES), lambda i: (CB_GB // 8, i, 0)),
            pl.BlockSpec(wa.shape, const),
            pl.BlockSpec(wb.shape, const),
            pl.BlockSpec(wo.shape, const),
            pl.BlockSpec((1, D_MODEL), const),
        ],
        out_specs=pl.BlockSpec((MIX_TM, D_MODEL), lambda i: (i, 0)),
        out_shape=jax.ShapeDtypeStruct((m, D_MODEL), jnp.float32),
        compiler_params=pltpu.CompilerParams(
            dimension_semantics=("arbitrary",),
            vmem_limit_bytes=VMEM_LIMIT),
        name="mix_out",
    )(x2, ya, yb, proj, proj, wa, wb, wo, nw)


FFN_TM = 512
FFN_CHUNK = 256
FFN_NCHUNK = D_FF // FFN_CHUNK


def _ffn_kernel(x_ref, nw1_ref, wg_ref, wu_ref, wd_ref, nw2_ref, o_ref):
    x = x_ref[...]
    h = _rms(x, nw1_ref[...]).astype(jnp.bfloat16)
    f = jnp.zeros((FFN_TM, D_MODEL), jnp.float32)
    for c in range(FFN_NCHUNK):
        sl = slice(c * FFN_CHUNK, (c + 1) * FFN_CHUNK)
        g = jnp.dot(h, wg_ref[:, sl], preferred_element_type=jnp.float32)
        u = jnp.dot(h, wu_ref[:, sl], preferred_element_type=jnp.float32)
        a = (jax.nn.silu(g) * u).astype(jnp.bfloat16)
        f = f + jnp.dot(a, wd_ref[sl, :], preferred_element_type=jnp.float32)
    o_ref[...] = x + _rms(f, nw2_ref[...])


def _ffn(x1, nw1, wg, wu, wd, nw2):
    m = x1.shape[0]
    const = lambda i: (0, 0)
    once = pl.Buffered(1)
    return pl.pallas_call(
        _ffn_kernel,
        grid=(m // FFN_TM,),
        in_specs=[
            pl.BlockSpec((FFN_TM, D_MODEL), lambda i: (i, 0)),
            pl.BlockSpec((1, D_MODEL), const),
            pl.BlockSpec(wg.shape, const, pipeline_mode=once),
            pl.BlockSpec(wu.shape, const, pipeline_mode=once),
            pl.BlockSpec(wd.shape, const, pipeline_mode=once),
            pl.BlockSpec((1, D_MODEL), const),
        ],
        out_specs=pl.BlockSpec((FFN_TM, D_MODEL), lambda i: (i, 0)),
        out_shape=jax.ShapeDtypeStruct((m, D_MODEL), jnp.float32),
        compiler_params=pltpu.CompilerParams(
            dimension_semantics=("arbitrary",),
            vmem_limit_bytes=VMEM_LIMIT),
        name="ffn",
    )(x1, nw1, wg, wu, wd, nw2)


def kernel(x, pre_mix_w, w_in, b_gate, lambda_q1, lambda_k1, lambda_q2, lambda_k2, subln_w, rpb,
           w_branch_a, w_branch_b, w_out, post_mix_w, pre_ffn_w, w_gate, w_up, w_down, post_ffn_w):
    batch, seq, _ = x.shape
    depth = w_in.shape[0]
    assert depth == 1, "LAMBDA_INIT is specialised to a single layer"
    bf = jnp.bfloat16
    f32 = jnp.float32
    slopes = jnp.asarray([2.0 ** (-8.0 * (i + 1) / DA_HEADS) for i in range(DA_HEADS)], f32)
    x2 = x.reshape(batch * seq, D_MODEL)
    for l in range(depth):
        b_full = jnp.concatenate(
            [jnp.zeros((CB_GA * LANES,), f32), b_gate[l].astype(f32)]).reshape(1, IN_COLS)
        proj = _in_proj(x2, pre_mix_w[l].reshape(1, D_MODEL).astype(f32), w_in[l].astype(bf), b_full)
        lam_vecs = jnp.stack([lambda_q1[l], lambda_k1[l], lambda_q2[l], lambda_k2[l]]).astype(f32)
        ya = _diff_attn(proj, slopes, lam_vecs, subln_w[l].reshape(1, 2 * HEAD_DIM).astype(f32), batch, seq)
        yb = _nbr_attn(proj, _nbr_bias_table(rpb[l]), batch, seq)
        x1 = _mix_out(x2, ya, yb, proj, w_branch_a[l].astype(bf), w_branch_b[l].astype(bf),
                      w_out[l].astype(bf), post_mix_w[l].reshape(1, D_MODEL).astype(f32))
        x2 = _ffn(x1, pre_ffn_w[l].reshape(1, D_MODEL).astype(f32), w_gate[l].astype(bf),
                  w_up[l].astype(bf), w_down[l].astype(bf), post_ffn_w[l].reshape(1, D_MODEL).astype(f32))
    return x2.reshape(batch, seq, D_MODEL)
```

```python
import functools
import math

import jax
import jax.numpy as jnp
from jax import lax
from jax.experimental import pallas as pl
from jax.experimental.pallas import tpu as pltpu

D_MODEL = 1024
HEAD_DIM = 64
DA_HEADS = 4
DA_W = DA_HEADS * 2 * HEAD_DIM
NA_HEADS = 8
NA_PAIRS = NA_HEADS // 2
GRID_W = 64
NA_ROWS = 8
NA_COLS = 16
D_FF = 2816
IN_COLS = 5120
EPS = 1e-6
NEG = -1e30
LANES = 128
LOG2E = math.log2(math.e)
LAMBDA_INIT = 0.8 - 0.6 * math.exp(-0.3 * 0)

CB_KA, CB_QB, CB_KB, CB_VB, CB_GA, CB_GB = 0, 4, 8, 12, 16, 24
N_COLBLK_N = 32
RB_QA, RB_VA = 0, 4

VMEM_LIMIT = 56 * 1024 * 1024


def _rms(xf, w):
    return xf * lax.rsqrt(jnp.mean(xf * xf, axis=-1, keepdims=True) + EPS) * w


IN_TM = 512
IN_TN = 1024
IN_NBLK = IN_TN // LANES
GATE_J0 = (CB_GA * LANES) // IN_TN


def _in_proj_t_kernel(x_ref, nw_ref, w_ref, cs_ref, o_ref):
    h = _rms(x_ref[...], nw_ref[...]).astype(jnp.bfloat16)
    acc = jnp.dot(h, w_ref[...], preferred_element_type=jnp.float32) * cs_ref[...]
    for c in range(IN_NBLK):
        o_ref[c, 0] = acc[:, c * LANES:(c + 1) * LANES].T.astype(o_ref.dtype)


def _in_proj_t(x2, nw, w_bf, colscale):
    m = x2.shape[0]
    return pl.pallas_call(
        _in_proj_t_kernel,
        grid=(m // IN_TM,),
        in_specs=[
            pl.BlockSpec((IN_TM, D_MODEL), lambda i: (i, 0)),
            pl.BlockSpec((1, D_MODEL), lambda i: (0, 0)),
            pl.BlockSpec((D_MODEL, IN_TN), lambda i: (0, 0)),
            pl.BlockSpec((1, IN_TN), lambda i: (0, 0)),
        ],
        out_specs=pl.BlockSpec((IN_NBLK, 1, LANES, IN_TM), lambda i: (0, i, 0, 0)),
        out_shape=jax.ShapeDtypeStruct((IN_NBLK, m // IN_TM, LANES, IN_TM), jnp.bfloat16),
        compiler_params=pltpu.CompilerParams(
            dimension_semantics=("arbitrary",),
            vmem_limit_bytes=VMEM_LIMIT),
        name="in_proj_t",
    )(x2, nw, w_bf, colscale)


def _in_proj_n_kernel(x_ref, nw_ref, w_ref, b_ref, o_ref):
    j = pl.program_id(0)
    h = _rms(x_ref[...], nw_ref[...]).astype(jnp.bfloat16)
    acc = jnp.dot(h, w_ref[...], preferred_element_type=jnp.float32)

    def store(val):
        for c in range(IN_NBLK):
            o_ref[c] = val[:, c * LANES:(c + 1) * LANES].astype(o_ref.dtype)

    @pl.when(j < GATE_J0)
    def _():
        store(acc)

    @pl.when(j >= GATE_J0)
    def _():
        store(jax.nn.sigmoid(acc + b_ref[...]))


def _in_proj_n(x2, nw, w_bf, b_full):
    m = x2.shape[0]
    ncols = w_bf.shape[1]
    return pl.pallas_call(
        _in_proj_n_kernel,
        grid=(ncols // IN_TN, m // IN_TM),
        in_specs=[
            pl.BlockSpec((IN_TM, D_MODEL), lambda j, i: (i, 0)),
            pl.BlockSpec((1, D_MODEL), lambda j, i: (0, 0)),
            pl.BlockSpec((D_MODEL, IN_TN), lambda j, i: (0, j)),
            pl.BlockSpec((1, IN_TN), lambda j, i: (0, j)),
        ],
        out_specs=pl.BlockSpec((IN_NBLK, IN_TM, LANES), lambda j, i: (j, i, 0)),
        out_shape=jax.ShapeDtypeStruct((ncols // LANES, m, LANES), jnp.bfloat16),
        compiler_params=pltpu.CompilerParams(
            dimension_semantics=("arbitrary", "arbitrary"),
            vmem_limit_bytes=VMEM_LIMIT),
        name="in_proj_n",
    )(x2, nw, w_bf, b_full)


DA_TQ = 1024
DA_TK = 512
DA_QG = 256
DA_NG = DA_TQ // DA_QG
DA_QT = DA_TQ // IN_TM
DA_DIAG = DA_TQ // DA_TK
DA_NAUG = 9
BF16_EXACT_INT = 256
DA_AHEAD = 3


def _alibi_tables(slopes):
    f32, bf = jnp.float32, jnp.bfloat16
    s2 = slopes.astype(f32) * LOG2E
    c1 = s2.astype(bf)
    c2 = (s2 - c1.astype(f32)).astype(bf)
    c3 = (s2 - c1.astype(f32) - c2.astype(f32)).astype(bf)
    parts = jnp.stack([c1, c2, c3], axis=1).astype(f32)
    s2_used = parts[:, 0] + parts[:, 1] + parts[:, 2]
    dk = jnp.arange(DA_TK)
    dk_lo = (dk % BF16_EXACT_INT).astype(f32)
    dk_hi = (dk - dk % BF16_EXACT_INT).astype(f32)
    ones3 = jnp.ones((3,), f32)
    k_aug = jnp.concatenate([
        jnp.broadcast_to((dk_lo[:, None] * ones3)[None], (DA_HEADS, DA_TK, 3)),
        jnp.broadcast_to((dk_hi[:, None] * ones3)[None], (DA_HEADS, DA_TK, 3)),
        jnp.broadcast_to(parts[:, None, :], (DA_HEADS, DA_TK, 3)),
    ], axis=-1)
    pad_k = jnp.zeros((DA_HEADS, DA_TK, HEAD_DIM - DA_NAUG), f32)
    k_aug = jnp.concatenate([k_aug, pad_k], axis=-1)
    zeros_k = jnp.zeros((DA_HEADS, DA_TK, HEAD_DIM), f32)
    k_tab = jnp.stack([jnp.concatenate([zeros_k, k_aug], axis=-1),
                       jnp.concatenate([k_aug, zeros_k], axis=-1)], axis=1)
    dq = (jnp.arange(DA_TQ) % DA_QG).astype(f32)
    sig = jnp.asarray([1.0, -1.0], f32)
    q_par = sig[None, :, None, None] * parts[:, None, :, None] * jnp.ones((DA_TQ,), f32)
    q_off = -sig[None, :, None, None] * jnp.broadcast_to(dq, (DA_HEADS, 1, 3, DA_TQ))
    q_aug = jnp.concatenate([q_par, q_par, q_off], axis=2)
    q_aug = jnp.concatenate(
        [q_aug, jnp.zeros((DA_HEADS, 2, HEAD_DIM - DA_NAUG, DA_TQ), f32)], axis=2)
    return k_tab.astype(bf), q_aug.astype(bf), s2_used


def _diff_attn_kernel(s2_ref, q_ref, k_ref, v_ref, ktab_ref, qaug_ref, lam_ref, sw_ref, o_ref,
                      ka_ref, qm_ref, dist_ref, m_ref, l_ref, acc_ref, *, nkc):
    h = pl.program_id(1)
    qi = pl.program_id(2)
    slope2 = s2_ref[h]

    @pl.when(qi == 0)
    def _():
        lane = lax.broadcasted_iota(jnp.int32, (DA_TK, LANES), 1)
        for t in range(nkc):
            rows = slice(t * DA_TK, (t + 1) * DA_TK)
            kt = k_ref[rows, :]
            ka_ref[0, rows, :] = jnp.where(lane < HEAD_DIM, kt, ktab_ref[0])
            ka_ref[1, rows, :] = jnp.where(lane >= HEAD_DIM, kt, ktab_ref[1])

    for t in range(DA_QT):
        qt = q_ref[t]
        cols = slice(t * IN_TM, (t + 1) * IN_TM)
        for sg in range(2):
            qm_ref[sg, 0, :HEAD_DIM, cols] = qt[:HEAD_DIM]
            qm_ref[sg, 0, HEAD_DIM:, cols] = qaug_ref[sg, :, cols]
            qm_ref[sg, 1, :HEAD_DIM, cols] = qaug_ref[sg, :, cols]
            qm_ref[sg, 1, HEAD_DIM:, cols] = qt[HEAD_DIM:]
    dq = lax.broadcasted_iota(jnp.int32, (DA_TK, DA_QG), 1)
    dk = lax.broadcasted_iota(jnp.int32, (DA_TK, DA_QG), 0)
    dist_ref[...] = (dq - dk).astype(jnp.float32) * slope2
    m_ref[...] = jnp.full_like(m_ref, -jnp.inf)
    l_ref[...] = jnp.zeros_like(l_ref)
    acc_ref[...] = jnp.zeros_like(acc_ref)

    tiles = [(g, mp) for g in range(DA_NG) for mp in range(2)]

    def make_chunk(sg, diagonal):
        sign = 1.0 if sg == 0 else -1.0

        def chunk(kc, carry):
            k0 = pl.multiple_of(kc * DA_TK, DA_TK)
            vt = v_ref[kc]

            def origin(g):
                return (qi * DA_TQ + g * DA_QG - kc * DA_TK).astype(jnp.float32) * slope2

            def scores(g, mp):
                cols = slice(g * DA_QG, (g + 1) * DA_QG)
                st = jnp.dot(ka_ref[mp, pl.ds(k0, DA_TK), :], qm_ref[sg, mp, :, cols],
                             preferred_element_type=jnp.float32)
                if diagonal:
                    st = st + 2.0 * jnp.minimum(dist_ref[...] + origin(g), 0.0)
                return st

            ahead = [scores(*tiles[i]) for i in range(DA_AHEAD)]
            for ti, (g, mp) in enumerate(tiles):
                cols = slice(g * DA_QG, (g + 1) * DA_QG)
                st = ahead.pop(0)
                if ti + DA_AHEAD < len(tiles):
                    ahead.append(scores(*tiles[ti + DA_AHEAD]))
                shift = sign * origin(g)
                m_old = m_ref[mp, :, cols]
                m_new = jnp.maximum(m_old, jnp.max(st, axis=0, keepdims=True) - shift)
                alpha = jnp.exp2(m_old - m_new)
                p = jnp.exp2(st - (m_new + shift))
                l_ref[mp, :, cols] = alpha * l_ref[mp, :, cols] + jnp.sum(p, axis=0, keepdims=True)
                acc_ref[mp, :, cols] = alpha * acc_ref[mp, :, cols] + jnp.dot(
                    vt, p.astype(jnp.bfloat16), preferred_element_type=jnp.float32)
                m_ref[mp, :, cols] = m_new
            return carry

        return chunk

    d0 = qi * DA_DIAG
    lax.fori_loop(d0, d0 + DA_DIAG, make_chunk(0, True), 0)
    lax.fori_loop(0, d0, make_chunk(0, False), 0)
    lax.fori_loop(d0 + DA_DIAG, nkc, make_chunk(1, False), 0)

    lv = lam_ref[...]
    lam = (jnp.exp(jnp.sum(lv[0:1] * lv[1:2], axis=-1, keepdims=True))
           - jnp.exp(jnp.sum(lv[2:3] * lv[3:4], axis=-1, keepdims=True))
           + LAMBDA_INIT)
    o = acc_ref[0] / l_ref[0] - lam * (acc_ref[1] / l_ref[1])
    o = o * lax.rsqrt(jnp.mean(o * o, axis=0, keepdims=True) + EPS) * sw_ref[...] * (1.0 - LAMBDA_INIT)
    o_ref[...] = o.T.astype(o_ref.dtype)


def _diff_attn(proj_t, proj_n, slopes, lam_vecs, subln_col, batch, seq):
    nq = seq // DA_TQ
    nkc = seq // DA_TK
    m = batch * seq
    k_tab, q_aug, s2_used = _alibi_tables(slopes)
    grid_spec = pltpu.PrefetchScalarGridSpec(
        num_scalar_prefetch=1,
        grid=(batch, DA_HEADS, nq),
        in_specs=[
            pl.BlockSpec((None, DA_QT, LANES, IN_TM), lambda b, h, qi, s: (RB_QA + h, b * nq + qi, 0, 0)),
            pl.BlockSpec((None, seq, LANES), lambda b, h, qi, s: (CB_KA + h, b, 0)),
            pl.BlockSpec((None, nkc, LANES, DA_TK), lambda b, h, qi, s: (RB_VA + h, b, 0, 0)),
            pl.BlockSpec((None, 2, DA_TK, LANES), lambda b, h, qi, s: (h, 0, 0, 0)),
            pl.BlockSpec((None, 2, HEAD_DIM, DA_TQ), lambda b, h, qi, s: (h, 0, 0, 0)),
            pl.BlockSpec((4, HEAD_DIM), lambda b, h, qi, s: (0, 0)),
            pl.BlockSpec((2 * HEAD_DIM, 1), lambda b, h, qi, s: (0, 0)),
        ],
        out_specs=pl.BlockSpec((None, DA_TQ, LANES), lambda b, h, qi, s: (h, b * nq + qi, 0)),
        scratch_shapes=[
            pltpu.VMEM((2, seq, LANES), jnp.bfloat16),
            pltpu.VMEM((2, 2, LANES, DA_TQ), jnp.bfloat16),
            pltpu.VMEM((DA_TK, DA_QG), jnp.float32),
            pltpu.VMEM((2, 1, DA_TQ), jnp.float32),
            pltpu.VMEM((2, 1, DA_TQ), jnp.float32),
            pltpu.VMEM((2, LANES, DA_TQ), jnp.float32),
        ],
    )
    return pl.pallas_call(
        functools.partial(_diff_attn_kernel, nkc=nkc),
        grid_spec=grid_spec,
        out_shape=jax.ShapeDtypeStruct((DA_HEADS, m, LANES), jnp.bfloat16),
        compiler_params=pltpu.CompilerParams(
            dimension_semantics=("arbitrary", "arbitrary", "arbitrary"),
            vmem_limit_bytes=VMEM_LIMIT),
        name="diff_attn",
    )(s2_used, proj_t, proj_n, proj_t, k_tab, q_aug, lam_vecs, subln_col)


NA_GROUP = 8
NA_TOK = NA_GROUP * GRID_W
NA_WIN = NA_ROWS * GRID_W
NA_BIAS_ROWS = (2 * NA_ROWS - 1) * GRID_W


def _nbr_attn_kernel(q_ref, k_ref, v_ref, bias_ref, o_ref, *, rows):
    g = pl.program_id(2)
    lane_q = lax.broadcasted_iota(jnp.int32, (GRID_W, LANES), 1)
    dn_last = (((1,), (1,)), ((), ()))
    dn_first = (((0,), (0,)), ((), ()))
    for rr in range(NA_GROUP):
        r = g * NA_GROUP + rr
        rs = jnp.clip(r - NA_ROWS // 2, 0, rows - NA_ROWS)
        k0 = pl.multiple_of(rs * GRID_W, GRID_W)
        b0 = pl.multiple_of((NA_ROWS - 1 - (r - rs)) * GRID_W, GRID_W)
        kwin = k_ref[pl.ds(k0, NA_WIN), :]
        vwin = v_ref[pl.ds(k0, NA_WIN), :]
        bias = bias_ref[pl.ds(b0, NA_WIN), :]
        q = q_ref[rr * GRID_W:(rr + 1) * GRID_W, :] * jnp.bfloat16(HEAD_DIM ** -0.5)
        zero = jnp.zeros_like(q)
        qbd = jnp.concatenate([jnp.where(lane_q < HEAD_DIM, q, zero),
                               jnp.where(lane_q >= HEAD_DIM, q, zero)], axis=0)
        st = lax.dot_general(kwin, qbd, dn_last, preferred_element_type=jnp.float32) + bias
        mx = jnp.max(st, axis=0, keepdims=True)
        p = jnp.exp(st - mx)
        den = jnp.sum(p, axis=0, keepdims=True)
        p = p * (1.0 / den)
        full = lax.dot_general(p.astype(jnp.bfloat16), vwin, dn_first,
                               preferred_element_type=jnp.float32)
        o = jnp.where(lane_q < HEAD_DIM, full[:GRID_W], full[GRID_W:])
        o_ref[rr * GRID_W:(rr + 1) * GRID_W, :] = o.astype(o_ref.dtype)


def _nbr_attn(proj, bias_ext, batch, seq):
    rows = seq // GRID_W
    ng = seq // NA_TOK
    m = batch * seq
    return pl.pallas_call(
        functools.partial(_nbr_attn_kernel, rows=rows),
        grid=(batch, NA_PAIRS, ng),
        in_specs=[
            pl.BlockSpec((None, NA_TOK, LANES), lambda b, hp, g: (CB_QB + hp, b * ng + g, 0)),
            pl.BlockSpec((None, seq, LANES), lambda b, hp, g: (CB_KB + hp, b, 0)),
            pl.BlockSpec((None, seq, LANES), lambda b, hp, g: (CB_VB + hp, b, 0)),
            pl.BlockSpec((None, NA_BIAS_ROWS, LANES), lambda b, hp, g: (hp, 0, 0)),
        ],
        out_specs=pl.BlockSpec((None, NA_TOK, LANES), lambda b, hp, g: (hp, b * ng + g, 0)),
        out_shape=jax.ShapeDtypeStruct((NA_PAIRS, m, LANES), jnp.bfloat16),
        compiler_params=pltpu.CompilerParams(
            dimension_semantics=("arbitrary", "arbitrary", "arbitrary"),
            vmem_limit_bytes=VMEM_LIMIT),
        name="nbr_attn",
    )(proj, proj, proj, bias_ext)


def _nbr_bias_table(rpb_l):
    col = jnp.arange(GRID_W)
    col_start = jnp.clip(col - NA_COLS // 2, 0, GRID_W - NA_COLS)
    col_in = (col[None, :] >= col_start[:, None]) & (col[None, :] < col_start[:, None] + NA_COLS)
    col_off = jnp.clip(col[None, :] - col[:, None] + NA_COLS - 1, 0, 2 * NA_COLS - 2)
    t = rpb_l.astype(jnp.float32)[:, :, col_off]
    t = jnp.where(col_in[None, None], t, NEG)
    t = t.reshape(NA_PAIRS, 2, 2 * NA_ROWS - 1, GRID_W, GRID_W)
    t = jnp.transpose(t, (0, 2, 4, 1, 3))
    return t.reshape(NA_PAIRS, NA_BIAS_ROWS, LANES)


MIX_TM = 512


def _cat_blocks(ref):
    return jnp.concatenate([ref[c] for c in range(ref.shape[0])], axis=-1)


def _mix_out_kernel(x_ref, ya_ref, yb_ref, ga_ref, gb_ref, wa_ref, wb_ref, wo_ref, nw_ref, o_ref):
    ya = _cat_blocks(ya_ref)
    yb = _cat_blocks(yb_ref)
    ga = _cat_blocks(ga_ref).astype(jnp.float32)
    gb = _cat_blocks(gb_ref).astype(jnp.float32)
    merged = (ga * jnp.dot(ya, wa_ref[...], preferred_element_type=jnp.float32)
              + gb * jnp.dot(yb, wb_ref[...], preferred_element_type=jnp.float32))
    t = jnp.dot(merged.astype(jnp.bfloat16), wo_ref[...], preferred_element_type=jnp.float32)
    o_ref[...] = x_ref[...] + _rms(t, nw_ref[...])


def _mix_out(x2, ya, yb, proj, wa, wb, wo, nw):
    m = x2.shape[0]
    const = lambda i: (0, 0)
    gate_blk = D_MODEL // LANES
    return pl.pallas_call(
        _mix_out_kernel,
        grid=(m // MIX_TM,),
        in_specs=[
            pl.BlockSpec((MIX_TM, D_MODEL), lambda i: (i, 0)),
            pl.BlockSpec((DA_HEADS, MIX_TM, LANES), lambda i: (0, i, 0)),
            pl.BlockSpec((NA_PAIRS, MIX_TM, LANES), lambda i: (0, i, 0)),
            pl.BlockSpec((gate_blk, MIX_TM, LANES), lambda i: (CB_GA // gate_blk, i, 0)),
            pl.BlockSpec((gate_blk, MIX_TM, LANES), lambda i: (CB_GB // gate_blk, i, 0)),
            pl.BlockSpec(wa.shape, const),
            pl.BlockSpec(wb.shape, const),
            pl.BlockSpec(wo.shape, const),
            pl.BlockSpec((1, D_MODEL), const),
        ],
        out_specs=pl.BlockSpec((MIX_TM, D_MODEL), lambda i: (i, 0)),
        out_shape=jax.ShapeDtypeStruct((m, D_MODEL), jnp.float32),
        compiler_params=pltpu.CompilerParams(
            dimension_semantics=("arbitrary",),
            vmem_limit_bytes=VMEM_LIMIT),
        name="mix_out",
    )(x2, ya, yb, proj, proj, wa, wb, wo, nw)


FFN_TM = 512
FFN_CHUNK = 256
FFN_NCHUNK = D_FF // FFN_CHUNK


def _ffn_kernel(x_ref, nw1_ref, wg_ref, wu_ref, wd_ref, nw2_ref, o_ref):
    x = x_ref[...]
    h = _rms(x, nw1_ref[...]).astype(jnp.bfloat16)
    f = jnp.zeros((FFN_TM, D_MODEL), jnp.float32)
    for c in range(FFN_NCHUNK):
        sl = slice(c * FFN_CHUNK, (c + 1) * FFN_CHUNK)
        g = jnp.dot(h, wg_ref[:, sl], preferred_element_type=jnp.float32)
        u = jnp.dot(h, wu_ref[:, sl], preferred_element_type=jnp.float32)
        a = (jax.nn.silu(g) * u).astype(jnp.bfloat16)
        f = f + jnp.dot(a, wd_ref[sl, :], preferred_element_type=jnp.float32)
    o_ref[...] = x + _rms(f, nw2_ref[...])


def _ffn(x1, nw1, wg, wu, wd, nw2):
    m = x1.shape[0]
    const = lambda i: (0, 0)
    once = pl.Buffered(1)
    return pl.pallas_call(
        _ffn_kernel,
        grid=(m // FFN_TM,),
        in_specs=[
            pl.BlockSpec((FFN_TM, D_MODEL), lambda i: (i, 0)),
            pl.BlockSpec((1, D_MODEL), const),
            pl.BlockSpec(wg.shape, const, pipeline_mode=once),
            pl.BlockSpec(wu.shape, const, pipeline_mode=once),
            pl.BlockSpec(wd.shape, const, pipeline_mode=once),
            pl.BlockSpec((1, D_MODEL), const),
        ],
        out_specs=pl.BlockSpec((FFN_TM, D_MODEL), lambda i: (i, 0)),
        out_shape=jax.ShapeDtypeStruct((m, D_MODEL), jnp.float32),
        compiler_params=pltpu.CompilerParams(
            dimension_semantics=("arbitrary",),
            vmem_limit_bytes=VMEM_LIMIT),
        name="ffn",
    )(x1, nw1, wg, wu, wd, nw2)


def kernel(x, pre_mix_w, w_in, b_gate, lambda_q1, lambda_k1, lambda_q2, lambda_k2, subln_w, rpb,
           w_branch_a, w_branch_b, w_out, post_mix_w, pre_ffn_w, w_gate, w_up, w_down, post_ffn_w):
    batch, seq, _ = x.shape
    depth = w_in.shape[0]
    assert depth == 1, "LAMBDA_INIT is specialised to a single layer"
    bf = jnp.bfloat16
    f32 = jnp.float32
    slopes = jnp.asarray([2.0 ** (-8.0 * (i + 1) / DA_HEADS) for i in range(DA_HEADS)], f32)
    colscale_t = jnp.concatenate([jnp.full((DA_W,), HEAD_DIM ** -0.5 * LOG2E, f32),
                                  jnp.ones((DA_W,), f32)]).reshape(1, IN_TN)
    x2 = x.reshape(batch * seq, D_MODEL)
    for l in range(depth):
        w_l = w_in[l].astype(bf)
        w_t = jnp.concatenate([w_l[:, :DA_W], w_l[:, 2 * DA_W:3 * DA_W]], axis=1)
        w_n = jnp.concatenate([w_l[:, DA_W:2 * DA_W], w_l[:, 3 * DA_W:]], axis=1)
        nw0 = pre_mix_w[l].reshape(1, D_MODEL).astype(f32)
        b_full = jnp.concatenate(
            [jnp.zeros((CB_GA * LANES,), f32), b_gate[l].astype(f32)]).reshape(1, N_COLBLK_N * LANES)
        proj_t = _in_proj_t(x2, nw0, w_t, colscale_t)
        proj_n = _in_proj_n(x2, nw0, w_n, b_full)
        lam_vecs = jnp.stack([lambda_q1[l], lambda_k1[l], lambda_q2[l], lambda_k2[l]]).astype(f32)
        ya = _diff_attn(proj_t, proj_n, slopes, lam_vecs,
                        subln_w[l].reshape(2 * HEAD_DIM, 1).astype(f32), batch, seq)
        yb = _nbr_attn(proj_n, _nbr_bias_table(rpb[l]), batch, seq)
        x1 = _mix_out(x2, ya, yb, proj_n, w_branch_a[l].astype(bf), w_branch_b[l].astype(bf),
                      w_out[l].astype(bf), post_mix_w[l].reshape(1, D_MODEL).astype(f32))
        x2 = _ffn(x1, pre_ffn_w[l].reshape(1, D_MODEL).astype(f32), w_gate[l].astype(bf),
                  w_up[l].astype(bf), w_down[l].astype(bf), post_ffn_w[l].reshape(1, D_MODEL).astype(f32))
    return x2.reshape(batch, seq, D_MODEL)
```

```python
import functools
import math

import jax
import jax.numpy as jnp
from jax import lax
from jax.experimental import pallas as pl
from jax.experimental.pallas import tpu as pltpu

D_MODEL = 1024
HEAD_DIM = 64
DA_HEADS = 4
DA_W = DA_HEADS * 2 * HEAD_DIM
NA_HEADS = 8
NA_PAIRS = NA_HEADS // 2
GRID_W = 64
NA_ROWS = 8
NA_COLS = 16
D_FF = 2816
IN_COLS = 5120
EPS = 1e-6
NEG = -1e30
LANES = 128
LOG2E = math.log2(math.e)
LAMBDA_INIT = 0.8 - 0.6 * math.exp(-0.3 * 0)

CB_KA, CB_QB, CB_KB, CB_VB, CB_GA, CB_GB = 0, 4, 8, 12, 16, 24
N_COLBLK_N = 32
RB_QA, RB_VA = 0, 4

VMEM_LIMIT = 56 * 1024 * 1024


def _rms(xf, w):
    return xf * lax.rsqrt(jnp.mean(xf * xf, axis=-1, keepdims=True) + EPS) * w


IN_TM = 512
IN_TN = 1024
IN_NBLK = IN_TN // LANES
Q_SCALE = HEAD_DIM ** -0.5 * LOG2E
STAT_ROWS = 8


def _max_group_sqnorm(a, grp):
    ab = a.astype(jnp.bfloat16).astype(jnp.float32)
    n2 = jnp.dot((ab * ab).astype(jnp.bfloat16), grp, preferred_element_type=jnp.float32)
    return jnp.max(n2, axis=0, keepdims=True)


def _in_proj_kernel(x_ref, nw_ref, w_ref, b_ref, grp_ref, ot_ref, on_ref, st_ref):
    h = _rms(x_ref[...], nw_ref[...]).astype(jnp.bfloat16)

    def chunk(c):
        return jnp.dot(h, w_ref[:, c * IN_TN:(c + 1) * IN_TN], preferred_element_type=jnp.float32)

    def store_natural(c, val):
        for j in range(IN_NBLK):
            on_ref[(c - 1) * IN_NBLK + j] = val[:, j * LANES:(j + 1) * LANES].astype(on_ref.dtype)

    acc = chunk(0)
    qa = acc[:, :DA_W] * Q_SCALE
    for j in range(IN_NBLK // 2):
        ot_ref[RB_QA + j, 0] = qa[:, j * LANES:(j + 1) * LANES].T.astype(ot_ref.dtype)
        ot_ref[RB_VA + j, 0] = acc[:, DA_W + j * LANES:DA_W + (j + 1) * LANES].T.astype(ot_ref.dtype)
    q_stat = _max_group_sqnorm(qa, grp_ref[...])
    acc = chunk(1)
    k_stat = _max_group_sqnorm(acc[:, :DA_W], grp_ref[...])
    store_natural(1, jnp.concatenate([acc[:, :DA_W], acc[:, DA_W:] * Q_SCALE], axis=1))
    store_natural(2, chunk(2))
    for c in (3, 4):
        gb = b_ref[:, (c - 3) * IN_TN:(c - 2) * IN_TN]
        store_natural(c, jax.nn.sigmoid(chunk(c) + gb))
    st_ref[0] = jnp.concatenate(
        [k_stat, q_stat, jnp.zeros((STAT_ROWS - 2, LANES), jnp.float32)], axis=0)


def _in_proj(x2, nw, w_bf, b_gate_row, grp):
    m = x2.shape[0]
    nt = m // IN_TM
    const = lambda i: (0, 0)
    return pl.pallas_call(
        _in_proj_kernel,
        grid=(nt,),
        in_specs=[
            pl.BlockSpec((IN_TM, D_MODEL), lambda i: (i, 0)),
            pl.BlockSpec((1, D_MODEL), const),
            pl.BlockSpec((D_MODEL, IN_COLS), const, pipeline_mode=pl.Buffered(1)),
            pl.BlockSpec((1, 2 * D_MODEL), const),
            pl.BlockSpec((DA_W, LANES), const),
        ],
        out_specs=[
            pl.BlockSpec((IN_NBLK, 1, LANES, IN_TM), lambda i: (0, i, 0, 0)),
            pl.BlockSpec((N_COLBLK_N, IN_TM, LANES), lambda i: (0, i, 0)),
            pl.BlockSpec((1, STAT_ROWS, LANES), lambda i: (i, 0, 0)),
        ],
        out_shape=[
            jax.ShapeDtypeStruct((IN_NBLK, nt, LANES, IN_TM), jnp.bfloat16),
            jax.ShapeDtypeStruct((N_COLBLK_N, m, LANES), jnp.bfloat16),
            jax.ShapeDtypeStruct((nt, STAT_ROWS, LANES), jnp.float32),
        ],
        compiler_params=pltpu.CompilerParams(
            dimension_semantics=("arbitrary",),
            vmem_limit_bytes=VMEM_LIMIT),
        name="in_proj",
    )(x2, nw, w_bf, b_gate_row, grp)


DA_TQ = 1024
DA_TK = 512
DA_QG = 256
DA_NG = DA_TQ // DA_QG
DA_QT = DA_TQ // IN_TM
DA_DIAG = DA_TQ // DA_TK
DA_NAUG = 9
BF16_EXACT_INT = 256
DA_AHEAD = 3
DA_SKIP_LOG2 = 150.0
DA_NORM_MARGIN = 1.01


def _alibi_tables(slopes):
    f32, bf = jnp.float32, jnp.bfloat16
    s2 = slopes.astype(f32) * LOG2E
    c1 = s2.astype(bf)
    c2 = (s2 - c1.astype(f32)).astype(bf)
    c3 = (s2 - c1.astype(f32) - c2.astype(f32)).astype(bf)
    parts = jnp.stack([c1, c2, c3], axis=1).astype(f32)
    s2_used = parts[:, 0] + parts[:, 1] + parts[:, 2]
    dk = jnp.arange(DA_TK)
    dk_lo = (dk % BF16_EXACT_INT).astype(f32)
    dk_hi = (dk - dk % BF16_EXACT_INT).astype(f32)
    ones3 = jnp.ones((3,), f32)
    k_aug = jnp.concatenate([
        jnp.broadcast_to((dk_lo[:, None] * ones3)[None], (DA_HEADS, DA_TK, 3)),
        jnp.broadcast_to((dk_hi[:, None] * ones3)[None], (DA_HEADS, DA_TK, 3)),
        jnp.broadcast_to(parts[:, None, :], (DA_HEADS, DA_TK, 3)),
    ], axis=-1)
    pad_k = jnp.zeros((DA_HEADS, DA_TK, HEAD_DIM - DA_NAUG), f32)
    k_aug = jnp.concatenate([k_aug, pad_k], axis=-1)
    zeros_k = jnp.zeros((DA_HEADS, DA_TK, HEAD_DIM), f32)
    k_tab = jnp.stack([jnp.concatenate([zeros_k, k_aug], axis=-1),
                       jnp.concatenate([k_aug, zeros_k], axis=-1)], axis=1)
    dq = (jnp.arange(DA_TQ) % DA_QG).astype(f32)
    sig = jnp.asarray([1.0, -1.0], f32)
    q_par = sig[None, :, None, None] * parts[:, None, :, None] * jnp.ones((DA_TQ,), f32)
    q_off = -sig[None, :, None, None] * jnp.broadcast_to(dq, (DA_HEADS, 1, 3, DA_TQ))
    q_aug = jnp.concatenate([q_par, q_par, q_off], axis=2)
    q_aug = jnp.concatenate(
        [q_aug, jnp.zeros((DA_HEADS, 2, HEAD_DIM - DA_NAUG, DA_TQ), f32)], axis=2)
    return k_tab.astype(bf), q_aug.astype(bf), s2_used


def _diff_attn_kernel(s2_ref, bounds_ref, q_ref, k_ref, v_ref, ktab_ref, qaug_ref, lam_ref, sw_ref, o_ref,
                      ka_ref, qm_ref, dist_ref, m_ref, l_ref, acc_ref, *, nkc):
    h = pl.program_id(1)
    qi = pl.program_id(2)
    slope2 = s2_ref[h]

    @pl.when(qi == 0)
    def _():
        lane = lax.broadcasted_iota(jnp.int32, (DA_TK, LANES), 1)
        for t in range(nkc):
            rows = slice(t * DA_TK, (t + 1) * DA_TK)
            kt = k_ref[rows, :]
            ka_ref[0, rows, :] = jnp.where(lane < HEAD_DIM, kt, ktab_ref[0])
            ka_ref[1, rows, :] = jnp.where(lane >= HEAD_DIM, kt, ktab_ref[1])

    for t in range(DA_QT):
        qt = q_ref[t]
        cols = slice(t * IN_TM, (t + 1) * IN_TM)
        for sg in range(2):
            qm_ref[sg, 0, :HEAD_DIM, cols] = qt[:HEAD_DIM]
            qm_ref[sg, 0, HEAD_DIM:, cols] = qaug_ref[sg, :, cols]
            qm_ref[sg, 1, :HEAD_DIM, cols] = qaug_ref[sg, :, cols]
            qm_ref[sg, 1, HEAD_DIM:, cols] = qt[HEAD_DIM:]
    dq = lax.broadcasted_iota(jnp.int32, (DA_TK, DA_QG), 1)
    dk = lax.broadcasted_iota(jnp.int32, (DA_TK, DA_QG), 0)
    dist_ref[...] = (dq - dk).astype(jnp.float32) * slope2
    m_ref[...] = jnp.full_like(m_ref, -jnp.inf)
    l_ref[...] = jnp.zeros_like(l_ref)
    acc_ref[...] = jnp.zeros_like(acc_ref)

    tiles = [(sub, g, mp) for sub in range(DA_DIAG) for g in range(DA_NG) for mp in range(2)]

    def make_body(sg, diagonal):
        sign = 1.0 if sg == 0 else -1.0

        def body(kb, carry):
            def origin(sub, g):
                kc = kb * DA_DIAG + sub
                return (qi * DA_TQ + g * DA_QG - kc * DA_TK).astype(jnp.float32) * slope2

            def scores(sub, g, mp):
                cols = slice(g * DA_QG, (g + 1) * DA_QG)
                k0 = pl.multiple_of((kb * DA_DIAG + sub) * DA_TK, DA_TK)
                st = jnp.dot(ka_ref[mp, pl.ds(k0, DA_TK), :], qm_ref[sg, mp, :, cols],
                             preferred_element_type=jnp.float32)
                if diagonal:
                    st = st + 2.0 * jnp.minimum(dist_ref[...] + origin(sub, g), 0.0)
                return st

            ahead = [scores(*tiles[i]) for i in range(DA_AHEAD)]
            for ti, (sub, g, mp) in enumerate(tiles):
                cols = slice(g * DA_QG, (g + 1) * DA_QG)
                st = ahead.pop(0)
                if ti + DA_AHEAD < len(tiles):
                    ahead.append(scores(*tiles[ti + DA_AHEAD]))
                shift = sign * origin(sub, g)
                m_old = m_ref[mp, :, cols]
                m_new = jnp.maximum(m_old, jnp.max(st, axis=0, keepdims=True) - shift)
                alpha = jnp.exp2(m_old - m_new)
                p = jnp.exp2(st - (m_new + shift))
                l_ref[mp, :, cols] = alpha * l_ref[mp, :, cols] + jnp.sum(p, axis=0, keepdims=True)
                acc_ref[mp, :, cols] = alpha * acc_ref[mp, :, cols] + jnp.dot(
                    v_ref[kb * DA_DIAG + sub], p.astype(jnp.bfloat16), preferred_element_type=jnp.float32)
                m_ref[mp, :, cols] = m_new
            return carry

        return body

    bounds_at = ((pl.program_id(0) * DA_HEADS + h) * pl.num_programs(2) + qi) * 2
    lo = bounds_ref[bounds_at]
    hi = bounds_ref[bounds_at + 1]
    make_body(0, True)(qi, 0)
    lax.fori_loop(lo, qi, make_body(0, False), 0)
    lax.fori_loop(qi + 1, hi, make_body(1, False), 0)

    lv = lam_ref[...]
    lam = (jnp.exp(jnp.sum(lv[0:1] * lv[1:2], axis=-1, keepdims=True))
           - jnp.exp(jnp.sum(lv[2:3] * lv[3:4], axis=-1, keepdims=True))
           + LAMBDA_INIT)
    o = acc_ref[0] / l_ref[0] - lam * (acc_ref[1] / l_ref[1])
    o = o * lax.rsqrt(jnp.mean(o * o, axis=0, keepdims=True) + EPS) * sw_ref[...] * (1.0 - LAMBDA_INIT)
    o_ref[...] = o.T.astype(o_ref.dtype)


def _skip_bounds(stats, s2_used, batch, seq):
    nq, nkc = seq // DA_TQ, seq // DA_TK
    nkb = nkc // DA_DIAG
    norms = DA_NORM_MARGIN * jnp.sqrt(stats[:, :2, :2 * DA_HEADS])
    norms = norms.reshape(batch, nkc, 2, DA_HEADS, 2)
    kn = norms[:, :, 0]
    qn = norms[:, :, 1].reshape(batch, nq, DA_QT, DA_HEADS, 2).max(axis=2)
    kn_self = kn.reshape(batch, nq, DA_DIAG, DA_HEADS, 2).max(axis=2)
    q_lo = jnp.arange(nq) * DA_TQ
    k_lo = jnp.arange(nkc) * DA_TK
    gap = jnp.maximum(jnp.maximum(q_lo[:, None] - (k_lo[None, :] + DA_TK - 1),
                                  k_lo[None, :] - (q_lo[:, None] + DA_TQ - 1)), 0)
    reach = qn[:, :, None] * (kn[:, None, :] + kn_self[:, :, None])
    bound = reach - s2_used[None, None, None, :, None] * gap[None, :, :, None, None].astype(jnp.float32)
    need = jnp.any(~(bound < -DA_SKIP_LOG2), axis=-1) | (gap == 0)[None, :, :, None]
    need = need.reshape(batch, nq, nkb, DA_DIAG, DA_HEADS).any(axis=3)
    blk = jnp.arange(nkb)[None, None, :, None]
    lo = jnp.min(jnp.where(need, blk, nkb), axis=2)
    hi = jnp.max(jnp.where(need, blk + 1, 0), axis=2)
    bounds = jnp.stack([lo, hi], axis=-1)
    return jnp.transpose(bounds, (0, 2, 1, 3)).reshape(-1).astype(jnp.int32)


def _diff_attn(proj_t, proj_n, stats, slopes, lam_vecs, subln_col, batch, seq):
    nq = seq // DA_TQ
    nkc = seq // DA_TK
    m = batch * seq
    k_tab, q_aug, s2_used = _alibi_tables(slopes)
    bounds = _skip_bounds(stats, s2_used, batch, seq)
    grid_spec = pltpu.PrefetchScalarGridSpec(
        num_scalar_prefetch=2,
        grid=(batch, DA_HEADS, nq),
        in_specs=[
            pl.BlockSpec((None, DA_QT, LANES, IN_TM), lambda b, h, qi, s, bd: (RB_QA + h, b * nq + qi, 0, 0)),
            pl.BlockSpec((None, seq, LANES), lambda b, h, qi, s, bd: (CB_KA + h, b, 0)),
            pl.BlockSpec((None, nkc, LANES, DA_TK), lambda b, h, qi, s, bd: (RB_VA + h, b, 0, 0)),
            pl.BlockSpec((None, 2, DA_TK, LANES), lambda b, h, qi, s, bd: (h, 0, 0, 0)),
            pl.BlockSpec((None, 2, HEAD_DIM, DA_TQ), lambda b, h, qi, s, bd: (h, 0, 0, 0)),
            pl.BlockSpec((4, HEAD_DIM), lambda b, h, qi, s, bd: (0, 0)),
            pl.BlockSpec((2 * HEAD_DIM, 1), lambda b, h, qi, s, bd: (0, 0)),
        ],
        out_specs=pl.BlockSpec((None, DA_TQ, LANES), lambda b, h, qi, s, bd: (h, b * nq + qi, 0)),
        scratch_shapes=[
            pltpu.VMEM((2, seq, LANES), jnp.bfloat16),
            pltpu.VMEM((2, 2, LANES, DA_TQ), jnp.bfloat16),
            pltpu.VMEM((DA_TK, DA_QG), jnp.float32),
            pltpu.VMEM((2, 1, DA_TQ), jnp.float32),
            pltpu.VMEM((2, 1, DA_TQ), jnp.float32),
            pltpu.VMEM((2, LANES, DA_TQ), jnp.float32),
        ],
    )
    return pl.pallas_call(
        functools.partial(_diff_attn_kernel, nkc=nkc),
        grid_spec=grid_spec,
        out_shape=jax.ShapeDtypeStruct((DA_HEADS, m, LANES), jnp.bfloat16),
        compiler_params=pltpu.CompilerParams(
            dimension_semantics=("arbitrary", "arbitrary", "arbitrary"),
            vmem_limit_bytes=VMEM_LIMIT),
        name="diff_attn",
    )(s2_used, bounds, proj_t, proj_n, proj_t, k_tab, q_aug, lam_vecs, subln_col)


NA_GROUP = 8
NA_TOK = NA_GROUP * GRID_W
NA_WIN = NA_ROWS * GRID_W
NA_AHEAD = 2
NA_BIAS_ROWS = (2 * NA_ROWS - 1) * GRID_W


def _nbr_attn_kernel(q_ref, k_ref, v_ref, bias_ref, o_ref, *, rows):
    g = pl.program_id(2)
    lane_q = lax.broadcasted_iota(jnp.int32, (GRID_W, LANES), 1)
    dn_last = (((1,), (1,)), ((), ()))
    dn_first = (((0,), (0,)), ((), ()))

    def key_start(rr):
        r = g * NA_GROUP + rr
        rs = jnp.clip(r - NA_ROWS // 2, 0, rows - NA_ROWS)
        return r, rs, pl.multiple_of(rs * GRID_W, GRID_W)

    def scores(rr):
        r, rs, k0 = key_start(rr)
        b0 = pl.multiple_of((NA_ROWS - 1 - (r - rs)) * GRID_W, GRID_W)
        q = q_ref[rr * GRID_W:(rr + 1) * GRID_W, :]
        zero = jnp.zeros_like(q)
        qbd = jnp.concatenate([jnp.where(lane_q < HEAD_DIM, q, zero),
                               jnp.where(lane_q >= HEAD_DIM, q, zero)], axis=0)
        return lax.dot_general(k_ref[pl.ds(k0, NA_WIN), :], qbd, dn_last,
                               preferred_element_type=jnp.float32) + bias_ref[pl.ds(b0, NA_WIN), :]

    ahead = [scores(rr) for rr in range(NA_AHEAD)]
    for rr in range(NA_GROUP):
        st = ahead.pop(0)
        if rr + NA_AHEAD < NA_GROUP:
            ahead.append(scores(rr + NA_AHEAD))
        mx = jnp.max(st, axis=0, keepdims=True)
        p = jnp.exp2(st - mx)
        den = jnp.sum(p, axis=0, keepdims=True)
        p = p * (1.0 / den)
        full = lax.dot_general(p.astype(jnp.bfloat16), v_ref[pl.ds(key_start(rr)[2], NA_WIN), :], dn_first,
                               preferred_element_type=jnp.float32)
        o = jnp.where(lane_q < HEAD_DIM, full[:GRID_W], full[GRID_W:])
        o_ref[rr * GRID_W:(rr + 1) * GRID_W, :] = o.astype(o_ref.dtype)


def _nbr_attn(proj, bias_ext, batch, seq):
    rows = seq // GRID_W
    ng = seq // NA_TOK
    m = batch * seq
    return pl.pallas_call(
        functools.partial(_nbr_attn_kernel, rows=rows),
        grid=(batch, NA_PAIRS, ng),
        in_specs=[
            pl.BlockSpec((None, NA_TOK, LANES), lambda b, hp, g: (CB_QB + hp, b * ng + g, 0)),
            pl.BlockSpec((None, seq, LANES), lambda b, hp, g: (CB_KB + hp, b, 0)),
            pl.BlockSpec((None, seq, LANES), lambda b, hp, g: (CB_VB + hp, b, 0)),
            pl.BlockSpec((None, NA_BIAS_ROWS, LANES), lambda b, hp, g: (hp, 0, 0)),
        ],
        out_specs=pl.BlockSpec((None, NA_TOK, LANES), lambda b, hp, g: (hp, b * ng + g, 0)),
        out_shape=jax.ShapeDtypeStruct((NA_PAIRS, m, LANES), jnp.bfloat16),
        compiler_params=pltpu.CompilerParams(
            dimension_semantics=("arbitrary", "arbitrary", "arbitrary"),
            vmem_limit_bytes=VMEM_LIMIT),
        name="nbr_attn",
    )(proj, proj, proj, bias_ext)


def _nbr_bias_table(rpb_l):
    col = jnp.arange(GRID_W)
    col_start = jnp.clip(col - NA_COLS // 2, 0, GRID_W - NA_COLS)
    col_in = (col[None, :] >= col_start[:, None]) & (col[None, :] < col_start[:, None] + NA_COLS)
    col_off = jnp.clip(col[None, :] - col[:, None] + NA_COLS - 1, 0, 2 * NA_COLS - 2)
    t = rpb_l.astype(jnp.float32)[:, :, col_off]
    t = jnp.where(col_in[None, None], t * LOG2E, NEG)
    t = t.reshape(NA_PAIRS, 2, 2 * NA_ROWS - 1, GRID_W, GRID_W)
    t = jnp.transpose(t, (0, 2, 4, 1, 3))
    return t.reshape(NA_PAIRS, NA_BIAS_ROWS, LANES)


MIX_TM = 512


def _cat_blocks(ref):
    return jnp.concatenate([ref[c] for c in range(ref.shape[0])], axis=-1)


def _mix_out_kernel(x_ref, ya_ref, yb_ref, ga_ref, gb_ref, wa_ref, wb_ref, wo_ref, nw_ref, o_ref):
    ya = _cat_blocks(ya_ref)
    yb = _cat_blocks(yb_ref)
    ga = _cat_blocks(ga_ref).astype(jnp.float32)
    gb = _cat_blocks(gb_ref).astype(jnp.float32)
    merged = (ga * jnp.dot(ya, wa_ref[...], preferred_element_type=jnp.float32)
              + gb * jnp.dot(yb, wb_ref[...], preferred_element_type=jnp.float32))
    t = jnp.dot(merged.astype(jnp.bfloat16), wo_ref[...], preferred_element_type=jnp.float32)
    o_ref[...] = x_ref[...] + _rms(t, nw_ref[...])


def _mix_out(x2, ya, yb, proj, wa, wb, wo, nw):
    m = x2.shape[0]
    const = lambda i: (0, 0)
    gate_blk = D_MODEL // LANES
    return pl.pallas_call(
        _mix_out_kernel,
        grid=(m // MIX_TM,),
        in_specs=[
            pl.BlockSpec((MIX_TM, D_MODEL), lambda i: (i, 0)),
            pl.BlockSpec((DA_HEADS, MIX_TM, LANES), lambda i: (0, i, 0)),
            pl.BlockSpec((NA_PAIRS, MIX_TM, LANES), lambda i: (0, i, 0)),
            pl.BlockSpec((gate_blk, MIX_TM, LANES), lambda i: (CB_GA // gate_blk, i, 0)),
            pl.BlockSpec((gate_blk, MIX_TM, LANES), lambda i: (CB_GB // gate_blk, i, 0)),
            pl.BlockSpec(wa.shape, const),
            pl.BlockSpec(wb.shape, const),
            pl.BlockSpec(wo.shape, const),
            pl.BlockSpec((1, D_MODEL), const),
        ],
        out_specs=pl.BlockSpec((MIX_TM, D_MODEL), lambda i: (i, 0)),
        out_shape=jax.ShapeDtypeStruct((m, D_MODEL), jnp.float32),
        compiler_params=pltpu.CompilerParams(
            dimension_semantics=("arbitrary",),
            vmem_limit_bytes=VMEM_LIMIT),
        name="mix_out",
    )(x2, ya, yb, proj, proj, wa, wb, wo, nw)


FFN_TM = 512
FFN_CHUNK = 256
FFN_NCHUNK = D_FF // FFN_CHUNK


def _ffn_kernel(x_ref, nw1_ref, wg_ref, wu_ref, wd_ref, nw2_ref, o_ref):
    x = x_ref[...]
    h = _rms(x, nw1_ref[...]).astype(jnp.bfloat16)
    f = jnp.zeros((FFN_TM, D_MODEL), jnp.float32)
    for c in range(FFN_NCHUNK):
        sl = slice(c * FFN_CHUNK, (c + 1) * FFN_CHUNK)
        g = jnp.dot(h, wg_ref[:, sl], preferred_element_type=jnp.float32)
        u = jnp.dot(h, wu_ref[:, sl], preferred_element_type=jnp.float32)
        a = (jax.nn.silu(g) * u).astype(jnp.bfloat16)
        f = f + jnp.dot(a, wd_ref[sl, :], preferred_element_type=jnp.float32)
    o_ref[...] = x + _rms(f, nw2_ref[...])


def _ffn(x1, nw1, wg, wu, wd, nw2):
    m = x1.shape[0]
    const = lambda i: (0, 0)
    once = pl.Buffered(1)
    return pl.pallas_call(
        _ffn_kernel,
        grid=(m // FFN_TM,),
        in_specs=[
            pl.BlockSpec((FFN_TM, D_MODEL), lambda i: (i, 0)),
            pl.BlockSpec((1, D_MODEL), const),
            pl.BlockSpec(wg.shape, const, pipeline_mode=once),
            pl.BlockSpec(wu.shape, const, pipeline_mode=once),
            pl.BlockSpec(wd.shape, const, pipeline_mode=once),
            pl.BlockSpec((1, D_MODEL), const),
        ],
        out_specs=pl.BlockSpec((FFN_TM, D_MODEL), lambda i: (i, 0)),
        out_shape=jax.ShapeDtypeStruct((m, D_MODEL), jnp.float32),
        compiler_params=pltpu.CompilerParams(
            dimension_semantics=("arbitrary",),
            vmem_limit_bytes=VMEM_LIMIT),
        name="ffn",
    )(x1, nw1, wg, wu, wd, nw2)


def kernel(x, pre_mix_w, w_in, b_gate, lambda_q1, lambda_k1, lambda_q2, lambda_k2, subln_w, rpb,
           w_branch_a, w_branch_b, w_out, post_mix_w, pre_ffn_w, w_gate, w_up, w_down, post_ffn_w):
    batch, seq, _ = x.shape
    depth = w_in.shape[0]
    assert depth == 1, "LAMBDA_INIT is specialised to a single layer"
    bf = jnp.bfloat16
    f32 = jnp.float32
    slopes = jnp.asarray([2.0 ** (-8.0 * (i + 1) / DA_HEADS) for i in range(DA_HEADS)], f32)
    grp = (jnp.arange(DA_W)[:, None] // HEAD_DIM == jnp.arange(LANES)[None, :]).astype(bf)
    x2 = x.reshape(batch * seq, D_MODEL)
    for l in range(depth):
        w_l = w_in[l].astype(bf)
        w_p = jnp.concatenate([w_l[:, :DA_W], w_l[:, 2 * DA_W:3 * DA_W],
                               w_l[:, DA_W:2 * DA_W], w_l[:, 3 * DA_W:]], axis=1)
        proj_t, proj_n, stats = _in_proj(x2, pre_mix_w[l].reshape(1, D_MODEL).astype(f32), w_p,
                                         b_gate[l].reshape(1, 2 * D_MODEL).astype(f32), grp)
        lam_vecs = jnp.stack([lambda_q1[l], lambda_k1[l], lambda_q2[l], lambda_k2[l]]).astype(f32)
        ya = _diff_attn(proj_t, proj_n, stats, slopes, lam_vecs,
                        subln_w[l].reshape(2 * HEAD_DIM, 1).astype(f32), batch, seq)
        yb = _nbr_attn(proj_n, _nbr_bias_table(rpb[l]), batch, seq)
        x1 = _mix_out(x2, ya, yb, proj_n, w_branch_a[l].astype(bf), w_branch_b[l].astype(bf),
                      w_out[l].astype(bf), post_mix_w[l].reshape(1, D_MODEL).astype(f32))
        x2 = _ffn(x1, pre_ffn_w[l].reshape(1, D_MODEL).astype(f32), w_gate[l].astype(bf),
                  w_up[l].astype(bf), w_down[l].astype(bf), post_ffn_w[l].reshape(1, D_MODEL).astype(f32))
    return x2.reshape(batch, seq, D_MODEL)
```

```python
import functools
import math

import jax
import jax.numpy as jnp
from jax import lax
from jax.experimental import pallas as pl
from jax.experimental.pallas import tpu as pltpu

D_MODEL = 1024
HEAD_DIM = 64
DA_HEADS = 4
DA_W = DA_HEADS * 2 * HEAD_DIM
NA_HEADS = 8
NA_PAIRS = NA_HEADS // 2
GRID_W = 64
NA_ROWS = 8
NA_COLS = 16
D_FF = 2816
IN_COLS = 5120
EPS = 1e-6
NEG = -1e30
LANES = 128
LOG2E = math.log2(math.e)
LAMBDA_INIT = 0.8 - 0.6 * math.exp(-0.3 * 0)

CB_KA, CB_QB, CB_KB, CB_VB, CB_GA, CB_GB = 0, 4, 8, 12, 16, 24
N_COLBLK_N = 32
RB_QA, RB_VA = 0, 4

VMEM_LIMIT = 56 * 1024 * 1024


def _rms(xf, w):
    return xf * lax.rsqrt(jnp.mean(xf * xf, axis=-1, keepdims=True) + EPS) * w


IN_TM = 512
IN_TN = 1024
IN_NBLK = IN_TN // LANES
Q_SCALE = HEAD_DIM ** -0.5 * LOG2E
STAT_ROWS = 8


def _max_group_sqnorm(a, grp):
    ab = a.astype(jnp.bfloat16).astype(jnp.float32)
    n2 = jnp.dot((ab * ab).astype(jnp.bfloat16), grp, preferred_element_type=jnp.float32)
    return jnp.max(n2, axis=0, keepdims=True)


def _in_proj_kernel(x_ref, nw_ref, w_ref, b_ref, grp_ref, ot_ref, on_ref, st_ref):
    h = _rms(x_ref[...], nw_ref[...]).astype(jnp.bfloat16)

    def chunk(c):
        return jnp.dot(h, w_ref[:, c * IN_TN:(c + 1) * IN_TN], preferred_element_type=jnp.float32)

    def store_natural(c, val):
        for j in range(IN_NBLK):
            on_ref[(c - 1) * IN_NBLK + j] = val[:, j * LANES:(j + 1) * LANES].astype(on_ref.dtype)

    acc = chunk(0)
    qa = acc[:, :DA_W] * Q_SCALE
    for j in range(IN_NBLK // 2):
        ot_ref[RB_QA + j, 0] = qa[:, j * LANES:(j + 1) * LANES].T.astype(ot_ref.dtype)
        ot_ref[RB_VA + j, 0] = acc[:, DA_W + j * LANES:DA_W + (j + 1) * LANES].T.astype(ot_ref.dtype)
    q_stat = _max_group_sqnorm(qa, grp_ref[...])
    acc = chunk(1)
    k_stat = _max_group_sqnorm(acc[:, :DA_W], grp_ref[...])
    store_natural(1, jnp.concatenate([acc[:, :DA_W], acc[:, DA_W:] * Q_SCALE], axis=1))
    store_natural(2, chunk(2))
    for c in (3, 4):
        gb = b_ref[:, (c - 3) * IN_TN:(c - 2) * IN_TN]
        store_natural(c, jax.nn.sigmoid(chunk(c) + gb))
    st_ref[0] = jnp.concatenate(
        [k_stat, q_stat, jnp.zeros((STAT_ROWS - 2, LANES), jnp.float32)], axis=0)


def _in_proj(x2, nw, w_bf, b_gate_row, grp):
    m = x2.shape[0]
    nt = m // IN_TM
    const = lambda i: (0, 0)
    return pl.pallas_call(
        _in_proj_kernel,
        grid=(nt,),
        in_specs=[
            pl.BlockSpec((IN_TM, D_MODEL), lambda i: (i, 0)),
            pl.BlockSpec((1, D_MODEL), const),
            pl.BlockSpec((D_MODEL, IN_COLS), const, pipeline_mode=pl.Buffered(1)),
            pl.BlockSpec((1, 2 * D_MODEL), const),
            pl.BlockSpec((DA_W, LANES), const),
        ],
        out_specs=[
            pl.BlockSpec((IN_NBLK, 1, LANES, IN_TM), lambda i: (0, i, 0, 0)),
            pl.BlockSpec((N_COLBLK_N, IN_TM, LANES), lambda i: (0, i, 0)),
            pl.BlockSpec((1, STAT_ROWS, LANES), lambda i: (i, 0, 0)),
        ],
        out_shape=[
            jax.ShapeDtypeStruct((IN_NBLK, nt, LANES, IN_TM), jnp.bfloat16),
            jax.ShapeDtypeStruct((N_COLBLK_N, m, LANES), jnp.bfloat16),
            jax.ShapeDtypeStruct((nt, STAT_ROWS, LANES), jnp.float32),
        ],
        compiler_params=pltpu.CompilerParams(
            dimension_semantics=("arbitrary",),
            vmem_limit_bytes=VMEM_LIMIT),
        name="in_proj",
    )(x2, nw, w_bf, b_gate_row, grp)


DA_TQ = 1024
DA_TK = 512
DA_QG = 256
DA_NG = DA_TQ // DA_QG
DA_QT = DA_TQ // IN_TM
DA_DIAG = DA_TQ // DA_TK
DA_NAUG = 9
BF16_EXACT_INT = 256
DA_AHEAD = 4
DA_SKIP_LOG2 = 127.0
DA_FROZEN_REACH = 64.0
DA_NBOUND = 3
DA_NORM_MARGIN = 1.01


def _alibi_tables(slopes):
    f32, bf = jnp.float32, jnp.bfloat16
    s2 = slopes.astype(f32) * LOG2E
    c1 = s2.astype(bf)
    c2 = (s2 - c1.astype(f32)).astype(bf)
    c3 = (s2 - c1.astype(f32) - c2.astype(f32)).astype(bf)
    parts = jnp.stack([c1, c2, c3], axis=1).astype(f32)
    s2_used = parts[:, 0] + parts[:, 1] + parts[:, 2]
    dk = jnp.arange(DA_TK)
    dk_lo = (dk % BF16_EXACT_INT).astype(f32)
    dk_hi = (dk - dk % BF16_EXACT_INT).astype(f32)
    ones3 = jnp.ones((3,), f32)
    k_aug = jnp.concatenate([
        jnp.broadcast_to((dk_lo[:, None] * ones3)[None], (DA_HEADS, DA_TK, 3)),
        jnp.broadcast_to((dk_hi[:, None] * ones3)[None], (DA_HEADS, DA_TK, 3)),
        jnp.broadcast_to(parts[:, None, :], (DA_HEADS, DA_TK, 3)),
    ], axis=-1)
    pad_k = jnp.zeros((DA_HEADS, DA_TK, HEAD_DIM - DA_NAUG), f32)
    k_aug = jnp.concatenate([k_aug, pad_k], axis=-1)
    zeros_k = jnp.zeros((DA_HEADS, DA_TK, HEAD_DIM), f32)
    k_tab = jnp.stack([jnp.concatenate([zeros_k, k_aug], axis=-1),
                       jnp.concatenate([k_aug, zeros_k], axis=-1)], axis=1)
    dq = (jnp.arange(DA_TQ) % DA_QG).astype(f32)
    sig = jnp.asarray([1.0, -1.0], f32)
    q_par = sig[None, :, None, None] * parts[:, None, :, None] * jnp.ones((DA_TQ,), f32)
    q_off = -sig[None, :, None, None] * jnp.broadcast_to(dq, (DA_HEADS, 1, 3, DA_TQ))
    q_aug = jnp.concatenate([q_par, q_par, q_off], axis=2)
    q_aug = jnp.concatenate(
        [q_aug, jnp.zeros((DA_HEADS, 2, HEAD_DIM - DA_NAUG, DA_TQ), f32)], axis=2)
    return k_tab.astype(bf), q_aug.astype(bf), s2_used


def _diff_attn_kernel(s2_ref, bounds_ref, q_ref, k_ref, v_ref, ktab_ref, qaug_ref, lam_ref, sw_ref, o_ref,
                      ka_ref, qm_ref, dist_ref, m_ref, l_ref, acc_ref, *, nkc):
    h = pl.program_id(1)
    qi = pl.program_id(2)
    slope2 = s2_ref[h]

    @pl.when(qi == 0)
    def _():
        lane = lax.broadcasted_iota(jnp.int32, (DA_TK, LANES), 1)
        for t in range(nkc):
            rows = slice(t * DA_TK, (t + 1) * DA_TK)
            kt = k_ref[rows, :]
            ka_ref[0, rows, :] = jnp.where(lane < HEAD_DIM, kt, ktab_ref[0])
            ka_ref[1, rows, :] = jnp.where(lane >= HEAD_DIM, kt, ktab_ref[1])

    for t in range(DA_QT):
        qt = q_ref[t]
        cols = slice(t * IN_TM, (t + 1) * IN_TM)
        for sg in range(2):
            qm_ref[sg, 0, :HEAD_DIM, cols] = qt[:HEAD_DIM]
            qm_ref[sg, 0, HEAD_DIM:, cols] = qaug_ref[sg, :, cols]
            qm_ref[sg, 1, :HEAD_DIM, cols] = qaug_ref[sg, :, cols]
            qm_ref[sg, 1, HEAD_DIM:, cols] = qt[HEAD_DIM:]
    dq = lax.broadcasted_iota(jnp.int32, (DA_TK, DA_QG), 1)
    dk = lax.broadcasted_iota(jnp.int32, (DA_TK, DA_QG), 0)
    dist_ref[...] = (dq - dk).astype(jnp.float32) * slope2
    m_ref[...] = jnp.full_like(m_ref, -jnp.inf)
    l_ref[...] = jnp.zeros_like(l_ref)
    acc_ref[...] = jnp.zeros_like(acc_ref)

    tiles = [(sub, g, mp) for sub in range(DA_DIAG) for g in range(DA_NG) for mp in range(2)]

    def make_body(sg, diagonal, frozen_max=False):
        sign = 1.0 if sg == 0 else -1.0

        def body(kb, carry):
            def origin(sub, g):
                kc = kb * DA_DIAG + sub
                return (qi * DA_TQ + g * DA_QG - kc * DA_TK).astype(jnp.float32) * slope2

            def scores(sub, g, mp):
                cols = slice(g * DA_QG, (g + 1) * DA_QG)
                k0 = pl.multiple_of((kb * DA_DIAG + sub) * DA_TK, DA_TK)
                st = jnp.dot(ka_ref[mp, pl.ds(k0, DA_TK), :], qm_ref[sg, mp, :, cols],
                             preferred_element_type=jnp.float32)
                if diagonal:
                    st = st + 2.0 * jnp.minimum(dist_ref[...] + origin(sub, g), 0.0)
                return st

            ahead = [scores(*tiles[i]) for i in range(DA_AHEAD)]
            for ti, (sub, g, mp) in enumerate(tiles):
                cols = slice(g * DA_QG, (g + 1) * DA_QG)
                st = ahead.pop(0)
                if ti + DA_AHEAD < len(tiles):
                    ahead.append(scores(*tiles[ti + DA_AHEAD]))
                shift = sign * origin(sub, g)
                vt = v_ref[kb * DA_DIAG + sub]
                if frozen_max:
                    p = jnp.exp2(st - (m_ref[mp, :, cols] + shift))
                    l_ref[mp, :, cols] = l_ref[mp, :, cols] + jnp.sum(p, axis=0, keepdims=True)
                    acc_ref[mp, :, cols] = acc_ref[mp, :, cols] + jnp.dot(
                        vt, p.astype(jnp.bfloat16), preferred_element_type=jnp.float32)
                else:
                    m_old = m_ref[mp, :, cols]
                    m_new = jnp.maximum(m_old, jnp.max(st, axis=0, keepdims=True) - shift)
                    alpha = jnp.exp2(m_old - m_new)
                    p = jnp.exp2(st - (m_new + shift))
                    l_ref[mp, :, cols] = alpha * l_ref[mp, :, cols] + jnp.sum(p, axis=0, keepdims=True)
                    acc_ref[mp, :, cols] = alpha * acc_ref[mp, :, cols] + jnp.dot(
                        vt, p.astype(jnp.bfloat16), preferred_element_type=jnp.float32)
                    m_ref[mp, :, cols] = m_new
            return carry

        return body

    bounds_at = ((pl.program_id(0) * DA_HEADS + h) * pl.num_programs(2) + qi) * DA_NBOUND
    lo = bounds_ref[bounds_at]
    hi = bounds_ref[bounds_at + 1]
    frozen_ok = bounds_ref[bounds_at + 2]
    make_body(0, True)(qi, 0)

    @pl.when(frozen_ok == 1)
    def _():
        lax.fori_loop(lo, qi, make_body(0, False, True), 0)
        lax.fori_loop(qi + 1, hi, make_body(1, False, True), 0)

    @pl.when(frozen_ok != 1)
    def _():
        lax.fori_loop(lo, qi, make_body(0, False), 0)
        lax.fori_loop(qi + 1, hi, make_body(1, False), 0)

    lv = lam_ref[...]
    lam = (jnp.exp(jnp.sum(lv[0:1] * lv[1:2], axis=-1, keepdims=True))
           - jnp.exp(jnp.sum(lv[2:3] * lv[3:4], axis=-1, keepdims=True))
           + LAMBDA_INIT)
    o = acc_ref[0] / l_ref[0] - lam * (acc_ref[1] / l_ref[1])
    o = o * lax.rsqrt(jnp.mean(o * o, axis=0, keepdims=True) + EPS) * sw_ref[...] * (1.0 - LAMBDA_INIT)
    o_ref[...] = o.T.astype(o_ref.dtype)


def _skip_bounds(stats, s2_used, batch, seq):
    nq, nkc = seq // DA_TQ, seq // DA_TK
    nkb = nkc // DA_DIAG
    norms = DA_NORM_MARGIN * jnp.sqrt(stats[:, :2, :2 * DA_HEADS])
    norms = norms.reshape(batch, nkc, 2, DA_HEADS, 2)
    kn = norms[:, :, 0]
    qn = norms[:, :, 1].reshape(batch, nq, DA_QT, DA_HEADS, 2).max(axis=2)
    kn_self = kn.reshape(batch, nq, DA_DIAG, DA_HEADS, 2).max(axis=2)
    q_lo = jnp.arange(nq) * DA_TQ
    k_lo = jnp.arange(nkc) * DA_TK
    gap = jnp.maximum(jnp.maximum(q_lo[:, None] - (k_lo[None, :] + DA_TK - 1),
                                  k_lo[None, :] - (q_lo[:, None] + DA_TQ - 1)), 0)
    reach = qn[:, :, None] * (kn[:, None, :] + kn_self[:, :, None])
    bound = reach - s2_used[None, None, None, :, None] * gap[None, :, :, None, None].astype(jnp.float32)
    need = jnp.any(~(bound < -DA_SKIP_LOG2), axis=-1) | (gap == 0)[None, :, :, None]
    need = need.reshape(batch, nq, nkb, DA_DIAG, DA_HEADS).any(axis=3)
    blk = jnp.arange(nkb)[None, None, :, None]
    lo = jnp.min(jnp.where(need, blk, nkb), axis=2)
    hi = jnp.max(jnp.where(need, blk + 1, 0), axis=2)
    reach_blk = reach.reshape(batch, nq, nkb, DA_DIAG, DA_HEADS, 2).max(axis=(3, 5))
    visited = (blk >= lo[:, :, None]) & (blk < hi[:, :, None])
    frozen_ok = jnp.all(~visited | (reach_blk <= DA_FROZEN_REACH), axis=2)
    bounds = jnp.stack([lo, hi, frozen_ok.astype(lo.dtype)], axis=-1)
    return jnp.transpose(bounds, (0, 2, 1, 3)).reshape(-1).astype(jnp.int32)


def _diff_attn(proj_t, proj_n, stats, slopes, lam_vecs, subln_col, batch, seq):
    nq = seq // DA_TQ
    nkc = seq // DA_TK
    m = batch * seq
    k_tab, q_aug, s2_used = _alibi_tables(slopes)
    bounds = _skip_bounds(stats, s2_used, batch, seq)
    grid_spec = pltpu.PrefetchScalarGridSpec(
        num_scalar_prefetch=2,
        grid=(batch, DA_HEADS, nq),
        in_specs=[
            pl.BlockSpec((None, DA_QT, LANES, IN_TM), lambda b, h, qi, s, bd: (RB_QA + h, b * nq + qi, 0, 0)),
            pl.BlockSpec((None, seq, LANES), lambda b, h, qi, s, bd: (CB_KA + h, b, 0)),
            pl.BlockSpec((None, nkc, LANES, DA_TK), lambda b, h, qi, s, bd: (RB_VA + h, b, 0, 0)),
            pl.BlockSpec((None, 2, DA_TK, LANES), lambda b, h, qi, s, bd: (h, 0, 0, 0)),
            pl.BlockSpec((None, 2, HEAD_DIM, DA_TQ), lambda b, h, qi, s, bd: (h, 0, 0, 0)),
            pl.BlockSpec((4, HEAD_DIM), lambda b, h, qi, s, bd: (0, 0)),
            pl.BlockSpec((2 * HEAD_DIM, 1), lambda b, h, qi, s, bd: (0, 0)),
        ],
        out_specs=pl.BlockSpec((None, DA_TQ, LANES), lambda b, h, qi, s, bd: (h, b * nq + qi, 0)),
        scratch_shapes=[
            pltpu.VMEM((2, seq, LANES), jnp.bfloat16),
            pltpu.VMEM((2, 2, LANES, DA_TQ), jnp.bfloat16),
            pltpu.VMEM((DA_TK, DA_QG), jnp.float32),
            pltpu.VMEM((2, 1, DA_TQ), jnp.float32),
            pltpu.VMEM((2, 1, DA_TQ), jnp.float32),
            pltpu.VMEM((2, LANES, DA_TQ), jnp.float32),
        ],
    )
    return pl.pallas_call(
        functools.partial(_diff_attn_kernel, nkc=nkc),
        grid_spec=grid_spec,
        out_shape=jax.ShapeDtypeStruct((DA_HEADS, m, LANES), jnp.bfloat16),
        compiler_params=pltpu.CompilerParams(
            dimension_semantics=("arbitrary", "arbitrary", "arbitrary"),
            vmem_limit_bytes=VMEM_LIMIT),
        name="diff_attn",
    )(s2_used, bounds, proj_t, proj_n, proj_t, k_tab, q_aug, lam_vecs, subln_col)


NA_GROUP = 8
NA_TOK = NA_GROUP * GRID_W
NA_WIN = NA_ROWS * GRID_W
NA_AHEAD = 2
NA_BIAS_ROWS = (2 * NA_ROWS - 1) * GRID_W


def _nbr_attn_kernel(q_ref, k_ref, v_ref, bias_ref, o_ref, *, rows):
    g = pl.program_id(2)
    lane_q = lax.broadcasted_iota(jnp.int32, (GRID_W, LANES), 1)
    dn_last = (((1,), (1,)), ((), ()))
    dn_first = (((0,), (0,)), ((), ()))

    def key_start(rr):
        r = g * NA_GROUP + rr
        rs = jnp.clip(r - NA_ROWS // 2, 0, rows - NA_ROWS)
        return r, rs, pl.multiple_of(rs * GRID_W, GRID_W)

    def scores(rr):
        r, rs, k0 = key_start(rr)
        b0 = pl.multiple_of((NA_ROWS - 1 - (r - rs)) * GRID_W, GRID_W)
        q = q_ref[rr * GRID_W:(rr + 1) * GRID_W, :]
        zero = jnp.zeros_like(q)
        qbd = jnp.concatenate([jnp.where(lane_q < HEAD_DIM, q, zero),
                               jnp.where(lane_q >= HEAD_DIM, q, zero)], axis=0)
        return lax.dot_general(k_ref[pl.ds(k0, NA_WIN), :], qbd, dn_last,
                               preferred_element_type=jnp.float32) + bias_ref[pl.ds(b0, NA_WIN), :]

    ahead = [scores(rr) for rr in range(NA_AHEAD)]
    for rr in range(NA_GROUP):
        st = ahead.pop(0)
        if rr + NA_AHEAD < NA_GROUP:
            ahead.append(scores(rr + NA_AHEAD))
        mx = jnp.max(st, axis=0, keepdims=True)
        p = jnp.exp2(st - mx)
        den = jnp.sum(p, axis=0, keepdims=True)
        p = p * (1.0 / den)
        full = lax.dot_general(p.astype(jnp.bfloat16), v_ref[pl.ds(key_start(rr)[2], NA_WIN), :], dn_first,
                               preferred_element_type=jnp.float32)
        o = jnp.where(lane_q < HEAD_DIM, full[:GRID_W], full[GRID_W:])
        o_ref[rr * GRID_W:(rr + 1) * GRID_W, :] = o.astype(o_ref.dtype)


def _nbr_attn(proj, bias_ext, batch, seq):
    rows = seq // GRID_W
    ng = seq // NA_TOK
    m = batch * seq
    return pl.pallas_call(
        functools.partial(_nbr_attn_kernel, rows=rows),
        grid=(batch, NA_PAIRS, ng),
        in_specs=[
            pl.BlockSpec((None, NA_TOK, LANES), lambda b, hp, g: (CB_QB + hp, b * ng + g, 0)),
            pl.BlockSpec((None, seq, LANES), lambda b, hp, g: (CB_KB + hp, b, 0)),
            pl.BlockSpec((None, seq, LANES), lambda b, hp, g: (CB_VB + hp, b, 0)),
            pl.BlockSpec((None, NA_BIAS_ROWS, LANES), lambda b, hp, g: (hp, 0, 0)),
        ],
        out_specs=pl.BlockSpec((None, NA_TOK, LANES), lambda b, hp, g: (hp, b * ng + g, 0)),
        out_shape=jax.ShapeDtypeStruct((NA_PAIRS, m, LANES), jnp.bfloat16),
        compiler_params=pltpu.CompilerParams(
            dimension_semantics=("arbitrary", "arbitrary", "arbitrary"),
            vmem_limit_bytes=VMEM_LIMIT),
        name="nbr_attn",
    )(proj, proj, proj, bias_ext)


def _nbr_bias_table(rpb_l):
    col = jnp.arange(GRID_W)
    col_start = jnp.clip(col - NA_COLS // 2, 0, GRID_W - NA_COLS)
    col_in = (col[None, :] >= col_start[:, None]) & (col[None, :] < col_start[:, None] + NA_COLS)
    col_off = jnp.clip(col[None, :] - col[:, None] + NA_COLS - 1, 0, 2 * NA_COLS - 2)
    t = rpb_l.astype(jnp.float32)[:, :, col_off]
    t = jnp.where(col_in[None, None], t * LOG2E, NEG)
    t = t.reshape(NA_PAIRS, 2, 2 * NA_ROWS - 1, GRID_W, GRID_W)
    t = jnp.transpose(t, (0, 2, 4, 1, 3))
    return t.reshape(NA_PAIRS, NA_BIAS_ROWS, LANES)


MIX_TM = 512


def _cat_blocks(ref):
    return jnp.concatenate([ref[c] for c in range(ref.shape[0])], axis=-1)


def _mix_out_kernel(x_ref, ya_ref, yb_ref, ga_ref, gb_ref, wa_ref, wb_ref, wo_ref, nw_ref, o_ref):
    ya = _cat_blocks(ya_ref)
    yb = _cat_blocks(yb_ref)
    ga = _cat_blocks(ga_ref).astype(jnp.float32)
    gb = _cat_blocks(gb_ref).astype(jnp.float32)
    merged = (ga * jnp.dot(ya, wa_ref[...], preferred_element_type=jnp.float32)
              + gb * jnp.dot(yb, wb_ref[...], preferred_element_type=jnp.float32))
    t = jnp.dot(merged.astype(jnp.bfloat16), wo_ref[...], preferred_element_type=jnp.float32)
    o_ref[...] = x_ref[...] + _rms(t, nw_ref[...])


def _mix_out(x2, ya, yb, proj, wa, wb, wo, nw):
    m = x2.shape[0]
    const = lambda i: (0, 0)
    gate_blk = D_MODEL // LANES
    return pl.pallas_call(
        _mix_out_kernel,
        grid=(m // MIX_TM,),
        in_specs=[
            pl.BlockSpec((MIX_TM, D_MODEL), lambda i: (i, 0)),
            pl.BlockSpec((DA_HEADS, MIX_TM, LANES), lambda i: (0, i, 0)),
            pl.BlockSpec((NA_PAIRS, MIX_TM, LANES), lambda i: (0, i, 0)),
            pl.BlockSpec((gate_blk, MIX_TM, LANES), lambda i: (CB_GA // gate_blk, i, 0)),
            pl.BlockSpec((gate_blk, MIX_TM, LANES), lambda i: (CB_GB // gate_blk, i, 0)),
            pl.BlockSpec(wa.shape, const),
            pl.BlockSpec(wb.shape, const),
            pl.BlockSpec(wo.shape, const),
            pl.BlockSpec((1, D_MODEL), const),
        ],
        out_specs=pl.BlockSpec((MIX_TM, D_MODEL), lambda i: (i, 0)),
        out_shape=jax.ShapeDtypeStruct((m, D_MODEL), jnp.float32),
        compiler_params=pltpu.CompilerParams(
            dimension_semantics=("arbitrary",),
            vmem_limit_bytes=VMEM_LIMIT),
        name="mix_out",
    )(x2, ya, yb, proj, proj, wa, wb, wo, nw)


FFN_TM = 512
FFN_CHUNK = 256
FFN_NCHUNK = D_FF // FFN_CHUNK


def _ffn_kernel(x_ref, nw1_ref, wg_ref, wu_ref, wd_ref, nw2_ref, o_ref):
    x = x_ref[...]
    h = _rms(x, nw1_ref[...]).astype(jnp.bfloat16)
    f = jnp.zeros((FFN_TM, D_MODEL), jnp.float32)
    for c in range(FFN_NCHUNK):
        sl = slice(c * FFN_CHUNK, (c + 1) * FFN_CHUNK)
        g = jnp.dot(h, wg_ref[:, sl], preferred_element_type=jnp.float32)
        u = jnp.dot(h, wu_ref[:, sl], preferred_element_type=jnp.float32)
        a = (jax.nn.silu(g) * u).astype(jnp.bfloat16)
        f = f + jnp.dot(a, wd_ref[sl, :], preferred_element_type=jnp.float32)
    o_ref[...] = x + _rms(f, nw2_ref[...])


def _ffn(x1, nw1, wg, wu, wd, nw2):
    m = x1.shape[0]
    const = lambda i: (0, 0)
    once = pl.Buffered(1)
    return pl.pallas_call(
        _ffn_kernel,
        grid=(m // FFN_TM,),
        in_specs=[
            pl.BlockSpec((FFN_TM, D_MODEL), lambda i: (i, 0)),
            pl.BlockSpec((1, D_MODEL), const),
            pl.BlockSpec(wg.shape, const, pipeline_mode=once),
            pl.BlockSpec(wu.shape, const, pipeline_mode=once),
            pl.BlockSpec(wd.shape, const, pipeline_mode=once),
            pl.BlockSpec((1, D_MODEL), const),
        ],
        out_specs=pl.BlockSpec((FFN_TM, D_MODEL), lambda i: (i, 0)),
        out_shape=jax.ShapeDtypeStruct((m, D_MODEL), jnp.float32),
        compiler_params=pltpu.CompilerParams(
            dimension_semantics=("arbitrary",),
            vmem_limit_bytes=VMEM_LIMIT),
        name="ffn",
    )(x1, nw1, wg, wu, wd, nw2)


def kernel(x, pre_mix_w, w_in, b_gate, lambda_q1, lambda_k1, lambda_q2, lambda_k2, subln_w, rpb,
           w_branch_a, w_branch_b, w_out, post_mix_w, pre_ffn_w, w_gate, w_up, w_down, post_ffn_w):
    batch, seq, _ = x.shape
    depth = w_in.shape[0]
    assert depth == 1, "LAMBDA_INIT is specialised to a single layer"
    bf = jnp.bfloat16
    f32 = jnp.float32
    slopes = jnp.asarray([2.0 ** (-8.0 * (i + 1) / DA_HEADS) for i in range(DA_HEADS)], f32)
    grp = (jnp.arange(DA_W)[:, None] // HEAD_DIM == jnp.arange(LANES)[None, :]).astype(bf)
    x2 = x.reshape(batch * seq, D_MODEL)
    for l in range(depth):
        w_l = w_in[l].astype(bf)
        w_p = jnp.concatenate([w_l[:, :DA_W], w_l[:, 2 * DA_W:3 * DA_W],
                               w_l[:, DA_W:2 * DA_W], w_l[:, 3 * DA_W:]], axis=1)
        proj_t, proj_n, stats = _in_proj(x2, pre_mix_w[l].reshape(1, D_MODEL).astype(f32), w_p,
                                         b_gate[l].reshape(1, 2 * D_MODEL).astype(f32), grp)
        lam_vecs = jnp.stack([lambda_q1[l], lambda_k1[l], lambda_q2[l], lambda_k2[l]]).astype(f32)
        ya = _diff_attn(proj_t, proj_n, stats, slopes, lam_vecs,
                        subln_w[l].reshape(2 * HEAD_DIM, 1).astype(f32), batch, seq)
        yb = _nbr_attn(proj_n, _nbr_bias_table(rpb[l]), batch, seq)
        x1 = _mix_out(x2, ya, yb, proj_n, w_branch_a[l].astype(bf), w_branch_b[l].astype(bf),
                      w_out[l].astype(bf), post_mix_w[l].reshape(1, D_MODEL).astype(f32))
        x2 = _ffn(x1, pre_ffn_w[l].reshape(1, D_MODEL).astype(f32), w_gate[l].astype(bf),
                  w_up[l].astype(bf), w_down[l].astype(bf), post_ffn_w[l].reshape(1, D_MODEL).astype(f32))
    return x2.reshape(batch, seq, D_MODEL)
```

```python
import functools
import math

import jax
import jax.numpy as jnp
from jax import lax
from jax.experimental import pallas as pl
from jax.experimental.pallas import tpu as pltpu

D_MODEL = 1024
HEAD_DIM = 64
DA_HEADS = 4
DA_W = DA_HEADS * 2 * HEAD_DIM
NA_HEADS = 8
NA_PAIRS = NA_HEADS // 2
NA_W = NA_HEADS * HEAD_DIM
GRID_W = 64
NA_ROWS = 8
NA_COLS = 16
D_FF = 2816
IN_COLS = 5120
EPS = 1e-6
NEG = -1e30
LANES = 128
LOG2E = math.log2(math.e)
LAMBDA_INIT = 0.8 - 0.6 * math.exp(-0.3 * 0)

CB_KA, CB_QB, CB_KB, CB_VB, CB_GA, CB_GB = 0, 4, 8, 12, 16, 24
N_COLBLK_N = 32
RB_QA, RB_VA = 0, 4

VMEM_LIMIT = 56 * 1024 * 1024


def _rms(xf, w):
    return xf * lax.rsqrt(jnp.mean(xf * xf, axis=-1, keepdims=True) + EPS) * w


IN_TM = 512
IN_TN = 1024
IN_NBLK = IN_TN // LANES
Q_SCALE = HEAD_DIM ** -0.5 * LOG2E
STAT_ROWS = 8


def _max_group_sqnorm(a, grp):
    ab = a.astype(jnp.bfloat16).astype(jnp.float32)
    n2 = jnp.dot((ab * ab).astype(jnp.bfloat16), grp, preferred_element_type=jnp.float32)
    return jnp.max(n2, axis=0, keepdims=True)


def _in_proj_kernel(x_ref, nw_ref, w_ref, b_ref, grp_ref, ot_ref, on_ref, st_ref):
    h = _rms(x_ref[...], nw_ref[...]).astype(jnp.bfloat16)

    def chunk(c):
        return jnp.dot(h, w_ref[:, c * IN_TN:(c + 1) * IN_TN], preferred_element_type=jnp.float32)

    def store_natural(c, val):
        for j in range(IN_NBLK):
            on_ref[(c - 1) * IN_NBLK + j] = val[:, j * LANES:(j + 1) * LANES].astype(on_ref.dtype)

    acc = chunk(0)
    qa = acc[:, :DA_W] * Q_SCALE
    for j in range(IN_NBLK // 2):
        ot_ref[RB_QA + j, 0] = qa[:, j * LANES:(j + 1) * LANES].T.astype(ot_ref.dtype)
        ot_ref[RB_VA + j, 0] = acc[:, DA_W + j * LANES:DA_W + (j + 1) * LANES].T.astype(ot_ref.dtype)
    q_stat = _max_group_sqnorm(qa, grp_ref[...])
    acc = chunk(1)
    k_stat = _max_group_sqnorm(acc[:, :DA_W], grp_ref[...])
    qb = acc[:, DA_W:] * Q_SCALE
    store_natural(1, jnp.concatenate([acc[:, :DA_W], qb], axis=1))
    qb_stat = _max_group_sqnorm(qb, grp_ref[...])
    acc = chunk(2)
    store_natural(2, acc)
    kb_stat = _max_group_sqnorm(acc[:, :NA_W], grp_ref[...])
    for c in (3, 4):
        gb = b_ref[:, (c - 3) * IN_TN:(c - 2) * IN_TN]
        store_natural(c, jax.nn.sigmoid(chunk(c) + gb))
    st_ref[0] = jnp.concatenate(
        [k_stat, q_stat, kb_stat, qb_stat, jnp.zeros((STAT_ROWS - 4, LANES), jnp.float32)], axis=0)


def _in_proj(x2, nw, w_bf, b_gate_row, grp):
    m = x2.shape[0]
    nt = m // IN_TM
    const = lambda i: (0, 0)
    return pl.pallas_call(
        _in_proj_kernel,
        grid=(nt,),
        in_specs=[
            pl.BlockSpec((IN_TM, D_MODEL), lambda i: (i, 0)),
            pl.BlockSpec((1, D_MODEL), const),
            pl.BlockSpec((D_MODEL, IN_COLS), const, pipeline_mode=pl.Buffered(1)),
            pl.BlockSpec((1, 2 * D_MODEL), const),
            pl.BlockSpec((DA_W, LANES), const),
        ],
        out_specs=[
            pl.BlockSpec((IN_NBLK, 1, LANES, IN_TM), lambda i: (0, i, 0, 0)),
            pl.BlockSpec((N_COLBLK_N, IN_TM, LANES), lambda i: (0, i, 0)),
            pl.BlockSpec((1, STAT_ROWS, LANES), lambda i: (i, 0, 0)),
        ],
        out_shape=[
            jax.ShapeDtypeStruct((IN_NBLK, nt, LANES, IN_TM), jnp.bfloat16),
            jax.ShapeDtypeStruct((N_COLBLK_N, m, LANES), jnp.bfloat16),
            jax.ShapeDtypeStruct((nt, STAT_ROWS, LANES), jnp.float32),
        ],
        compiler_params=pltpu.CompilerParams(
            dimension_semantics=("arbitrary",),
            vmem_limit_bytes=VMEM_LIMIT),
        name="in_proj",
    )(x2, nw, w_bf, b_gate_row, grp)


DA_TQ = 1024
DA_TK = 512
DA_QG = 256
DA_NG = DA_TQ // DA_QG
DA_QT = DA_TQ // IN_TM
DA_DIAG = DA_TQ // DA_TK
DA_NAUG = 9
BF16_EXACT_INT = 256
DA_AHEAD = 4
DA_SKIP_LOG2 = 127.0
DA_FIXED_REACH = 64.0
DA_NBOUND = 3
DA_NORM_MARGIN = 1.01


def _alibi_tables(slopes):
    f32, bf = jnp.float32, jnp.bfloat16
    s2 = slopes.astype(f32) * LOG2E
    c1 = s2.astype(bf)
    c2 = (s2 - c1.astype(f32)).astype(bf)
    c3 = (s2 - c1.astype(f32) - c2.astype(f32)).astype(bf)
    parts = jnp.stack([c1, c2, c3], axis=1).astype(f32)
    s2_used = parts[:, 0] + parts[:, 1] + parts[:, 2]
    dk = jnp.arange(DA_TK)
    dk_lo = (dk % BF16_EXACT_INT).astype(f32)
    dk_hi = (dk - dk % BF16_EXACT_INT).astype(f32)
    ones3 = jnp.ones((3,), f32)
    k_aug = jnp.concatenate([
        jnp.broadcast_to((dk_lo[:, None] * ones3)[None], (DA_HEADS, DA_TK, 3)),
        jnp.broadcast_to((dk_hi[:, None] * ones3)[None], (DA_HEADS, DA_TK, 3)),
        jnp.broadcast_to(parts[:, None, :], (DA_HEADS, DA_TK, 3)),
    ], axis=-1)
    pad_k = jnp.zeros((DA_HEADS, DA_TK, HEAD_DIM - DA_NAUG), f32)
    k_aug = jnp.concatenate([k_aug, pad_k], axis=-1)
    zeros_k = jnp.zeros((DA_HEADS, DA_TK, HEAD_DIM), f32)
    k_tab = jnp.stack([jnp.concatenate([zeros_k, k_aug], axis=-1),
                       jnp.concatenate([k_aug, zeros_k], axis=-1)], axis=1)
    dq = (jnp.arange(DA_TQ) % DA_QG).astype(f32)
    sig = jnp.asarray([1.0, -1.0], f32)
    q_par = sig[None, :, None, None] * parts[:, None, :, None] * jnp.ones((DA_TQ,), f32)
    q_off = -sig[None, :, None, None] * jnp.broadcast_to(dq, (DA_HEADS, 1, 3, DA_TQ))
    q_aug = jnp.concatenate([q_par, q_par, q_off], axis=2)
    q_aug = jnp.concatenate(
        [q_aug, jnp.zeros((DA_HEADS, 2, HEAD_DIM - DA_NAUG, DA_TQ), f32)], axis=2)
    return k_tab.astype(bf), q_aug.astype(bf), s2_used


def _diff_attn_kernel(s2_ref, bounds_ref, refpt_ref, q_ref, k_ref, v_ref, ktab_ref, qaug_ref, lam_ref, sw_ref, o_ref,
                      ka_ref, qm_ref, dist_ref, m_ref, l_ref, acc_ref, *, nkc):
    h = pl.program_id(1)
    qi = pl.program_id(2)
    slope2 = s2_ref[h]

    @pl.when(qi == 0)
    def _():
        lane = lax.broadcasted_iota(jnp.int32, (DA_TK, LANES), 1)
        for t in range(nkc):
            rows = slice(t * DA_TK, (t + 1) * DA_TK)
            kt = k_ref[rows, :]
            ka_ref[0, rows, :] = jnp.where(lane < HEAD_DIM, kt, ktab_ref[0])
            ka_ref[1, rows, :] = jnp.where(lane >= HEAD_DIM, kt, ktab_ref[1])

    for t in range(DA_QT):
        qt = q_ref[t]
        cols = slice(t * IN_TM, (t + 1) * IN_TM)
        for sg in range(2):
            qm_ref[sg, 0, :HEAD_DIM, cols] = qt[:HEAD_DIM]
            qm_ref[sg, 0, HEAD_DIM:, cols] = qaug_ref[sg, :, cols]
            qm_ref[sg, 1, :HEAD_DIM, cols] = qaug_ref[sg, :, cols]
            qm_ref[sg, 1, HEAD_DIM:, cols] = qt[HEAD_DIM:]
    dq = lax.broadcasted_iota(jnp.int32, (DA_TK, DA_QG), 1)
    dk = lax.broadcasted_iota(jnp.int32, (DA_TK, DA_QG), 0)
    dist_ref[...] = (dq - dk).astype(jnp.float32) * slope2
    m_ref[...] = jnp.full_like(m_ref, -jnp.inf)
    l_ref[...] = jnp.zeros_like(l_ref)
    acc_ref[...] = jnp.zeros_like(acc_ref)

    tiles = [(sub, g, mp) for sub in range(DA_DIAG) for g in range(DA_NG) for mp in range(2)]

    step = (pl.program_id(0) * DA_HEADS + h) * pl.num_programs(2) + qi

    def make_body(sg, diagonal, fixed_ref=False):
        sign = 1.0 if sg == 0 else -1.0

        def body(kb, carry):
            def origin(sub, g):
                kc = kb * DA_DIAG + sub
                return (qi * DA_TQ + g * DA_QG - kc * DA_TK).astype(jnp.float32) * slope2

            def scores(sub, g, mp):
                cols = slice(g * DA_QG, (g + 1) * DA_QG)
                k0 = pl.multiple_of((kb * DA_DIAG + sub) * DA_TK, DA_TK)
                st = jnp.dot(ka_ref[mp, pl.ds(k0, DA_TK), :], qm_ref[sg, mp, :, cols],
                             preferred_element_type=jnp.float32)
                if diagonal:
                    st = st + 2.0 * jnp.minimum(dist_ref[...] + origin(sub, g), 0.0)
                return st

            ahead = [scores(*tiles[i]) for i in range(DA_AHEAD)]
            for ti, (sub, g, mp) in enumerate(tiles):
                cols = slice(g * DA_QG, (g + 1) * DA_QG)
                st = ahead.pop(0)
                if ti + DA_AHEAD < len(tiles):
                    ahead.append(scores(*tiles[ti + DA_AHEAD]))
                shift = sign * origin(sub, g)
                vt = v_ref[kb * DA_DIAG + sub]
                if fixed_ref:
                    p = jnp.exp2(st - (refpt_ref[step * 2 + mp] + shift))
                    l_ref[mp, :, cols] = l_ref[mp, :, cols] + jnp.sum(p, axis=0, keepdims=True)
                    acc_ref[mp, :, cols] = acc_ref[mp, :, cols] + jnp.dot(
                        vt, p.astype(jnp.bfloat16), preferred_element_type=jnp.float32)
                else:
                    m_old = m_ref[mp, :, cols]
                    m_new = jnp.maximum(m_old, jnp.max(st, axis=0, keepdims=True) - shift)
                    alpha = jnp.exp2(m_old - m_new)
                    p = jnp.exp2(st - (m_new + shift))
                    l_ref[mp, :, cols] = alpha * l_ref[mp, :, cols] + jnp.sum(p, axis=0, keepdims=True)
                    acc_ref[mp, :, cols] = alpha * acc_ref[mp, :, cols] + jnp.dot(
                        vt, p.astype(jnp.bfloat16), preferred_element_type=jnp.float32)
                    m_ref[mp, :, cols] = m_new
            return carry

        return body

    lo = bounds_ref[step * DA_NBOUND]
    hi = bounds_ref[step * DA_NBOUND + 1]
    fixed_ok = bounds_ref[step * DA_NBOUND + 2]

    for fixed in (True, False):
        @pl.when((fixed_ok == 1) if fixed else (fixed_ok != 1))
        def _():
            make_body(0, True, fixed)(qi, 0)
            lax.fori_loop(lo, qi, make_body(0, False, fixed), 0)
            lax.fori_loop(qi + 1, hi, make_body(1, False, fixed), 0)

    lv = lam_ref[...]
    lam = (jnp.exp(jnp.sum(lv[0:1] * lv[1:2], axis=-1, keepdims=True))
           - jnp.exp(jnp.sum(lv[2:3] * lv[3:4], axis=-1, keepdims=True))
           + LAMBDA_INIT)
    o = acc_ref[0] / l_ref[0] - lam * (acc_ref[1] / l_ref[1])
    o = o * lax.rsqrt(jnp.mean(o * o, axis=0, keepdims=True) + EPS) * sw_ref[...] * (1.0 - LAMBDA_INIT)
    o_ref[...] = o.T.astype(o_ref.dtype)


def _skip_bounds(stats, s2_used, batch, seq):
    nq, nkc = seq // DA_TQ, seq // DA_TK
    nkb = nkc // DA_DIAG
    norms = DA_NORM_MARGIN * jnp.sqrt(stats[:, :2, :2 * DA_HEADS])
    norms = norms.reshape(batch, nkc, 2, DA_HEADS, 2)
    kn = norms[:, :, 0]
    qn = norms[:, :, 1].reshape(batch, nq, DA_QT, DA_HEADS, 2).max(axis=2)
    kn_self = kn.reshape(batch, nq, DA_DIAG, DA_HEADS, 2).max(axis=2)
    q_lo = jnp.arange(nq) * DA_TQ
    k_lo = jnp.arange(nkc) * DA_TK
    gap = jnp.maximum(jnp.maximum(q_lo[:, None] - (k_lo[None, :] + DA_TK - 1),
                                  k_lo[None, :] - (q_lo[:, None] + DA_TQ - 1)), 0)
    reach = qn[:, :, None] * (kn[:, None, :] + kn_self[:, :, None])
    bound = reach - s2_used[None, None, None, :, None] * gap[None, :, :, None, None].astype(jnp.float32)
    need = jnp.any(~(bound < -DA_SKIP_LOG2), axis=-1) | (gap == 0)[None, :, :, None]
    need = need.reshape(batch, nq, nkb, DA_DIAG, DA_HEADS).any(axis=3)
    blk = jnp.arange(nkb)[None, None, :, None]
    lo = jnp.min(jnp.where(need, blk, nkb), axis=2)
    hi = jnp.max(jnp.where(need, blk + 1, 0), axis=2)
    refpt = -(qn * kn_self)
    reach_blk = reach.reshape(batch, nq, nkb, DA_DIAG, DA_HEADS, 2).max(axis=(3, 5))
    visited = (blk >= lo[:, :, None]) & (blk < hi[:, :, None])
    fixed_ok = jnp.all(~visited | (reach_blk <= DA_FIXED_REACH), axis=2)
    bounds = jnp.stack([lo, hi, fixed_ok.astype(lo.dtype)], axis=-1)
    return (jnp.transpose(bounds, (0, 2, 1, 3)).reshape(-1).astype(jnp.int32),
            jnp.transpose(refpt, (0, 2, 1, 3)).reshape(-1).astype(jnp.float32))


def _diff_attn(proj_t, proj_n, stats, slopes, lam_vecs, subln_col, batch, seq):
    nq = seq // DA_TQ
    nkc = seq // DA_TK
    m = batch * seq
    k_tab, q_aug, s2_used = _alibi_tables(slopes)
    bounds, refpt = _skip_bounds(stats, s2_used, batch, seq)
    grid_spec = pltpu.PrefetchScalarGridSpec(
        num_scalar_prefetch=3,
        grid=(batch, DA_HEADS, nq),
        in_specs=[
            pl.BlockSpec((None, DA_QT, LANES, IN_TM), lambda b, h, qi, s, bd, rp: (RB_QA + h, b * nq + qi, 0, 0)),
            pl.BlockSpec((None, seq, LANES), lambda b, h, qi, s, bd, rp: (CB_KA + h, b, 0)),
            pl.BlockSpec((None, nkc, LANES, DA_TK), lambda b, h, qi, s, bd, rp: (RB_VA + h, b, 0, 0)),
            pl.BlockSpec((None, 2, DA_TK, LANES), lambda b, h, qi, s, bd, rp: (h, 0, 0, 0)),
            pl.BlockSpec((None, 2, HEAD_DIM, DA_TQ), lambda b, h, qi, s, bd, rp: (h, 0, 0, 0)),
            pl.BlockSpec((4, HEAD_DIM), lambda b, h, qi, s, bd, rp: (0, 0)),
            pl.BlockSpec((2 * HEAD_DIM, 1), lambda b, h, qi, s, bd, rp: (0, 0)),
        ],
        out_specs=pl.BlockSpec((None, DA_TQ, LANES), lambda b, h, qi, s, bd, rp: (h, b * nq + qi, 0)),
        scratch_shapes=[
            pltpu.VMEM((2, seq, LANES), jnp.bfloat16),
            pltpu.VMEM((2, 2, LANES, DA_TQ), jnp.bfloat16),
            pltpu.VMEM((DA_TK, DA_QG), jnp.float32),
            pltpu.VMEM((2, 1, DA_TQ), jnp.float32),
            pltpu.VMEM((2, 1, DA_TQ), jnp.float32),
            pltpu.VMEM((2, LANES, DA_TQ), jnp.float32),
        ],
    )
    return pl.pallas_call(
        functools.partial(_diff_attn_kernel, nkc=nkc),
        grid_spec=grid_spec,
        out_shape=jax.ShapeDtypeStruct((DA_HEADS, m, LANES), jnp.bfloat16),
        compiler_params=pltpu.CompilerParams(
            dimension_semantics=("arbitrary", "arbitrary", "arbitrary"),
            vmem_limit_bytes=VMEM_LIMIT),
        name="diff_attn",
    )(s2_used, bounds, refpt, proj_t, proj_n, proj_t, k_tab, q_aug, lam_vecs, subln_col)


NA_GROUP = 8
NA_TOK = NA_GROUP * GRID_W
NA_WIN = NA_ROWS * GRID_W
NA_FIXED_REACH = 64.0
NA_AHEAD = 2
NA_BIAS_ROWS = (2 * NA_ROWS - 1) * GRID_W


def _nbr_attn_kernel(ok_ref, refpt_ref, q_ref, k_ref, v_ref, bias_ref, o_ref, *, rows):
    g = pl.program_id(2)
    step = (pl.program_id(0) * NA_PAIRS + pl.program_id(1)) * pl.num_programs(2) + g
    lane_q = lax.broadcasted_iota(jnp.int32, (GRID_W, LANES), 1)
    dn_last = (((1,), (1,)), ((), ()))
    dn_first = (((0,), (0,)), ((), ()))
    ones = jnp.ones((NA_WIN, LANES), jnp.bfloat16)

    def key_start(rr):
        r = g * NA_GROUP + rr
        rs = jnp.clip(r - NA_ROWS // 2, 0, rows - NA_ROWS)
        return r, rs, pl.multiple_of(rs * GRID_W, GRID_W)

    def scores(rr):
        r, rs, k0 = key_start(rr)
        b0 = pl.multiple_of((NA_ROWS - 1 - (r - rs)) * GRID_W, GRID_W)
        q = q_ref[rr * GRID_W:(rr + 1) * GRID_W, :]
        zero = jnp.zeros_like(q)
        qbd = jnp.concatenate([jnp.where(lane_q < HEAD_DIM, q, zero),
                               jnp.where(lane_q >= HEAD_DIM, q, zero)], axis=0)
        return lax.dot_general(k_ref[pl.ds(k0, NA_WIN), :], qbd, dn_last,
                               preferred_element_type=jnp.float32) + bias_ref[pl.ds(b0, NA_WIN), :]

    def group(fixed_ref):
        if fixed_ref:
            lane_k = lax.broadcasted_iota(jnp.int32, (1, LANES), 1)
            ref_row = jnp.where(lane_k < HEAD_DIM, refpt_ref[step * 2], refpt_ref[step * 2 + 1])
        ahead = [scores(rr) for rr in range(NA_AHEAD)]
        for rr in range(NA_GROUP):
            st = ahead.pop(0)
            if rr + NA_AHEAD < NA_GROUP:
                ahead.append(scores(rr + NA_AHEAD))
            p = jnp.exp2(st - (ref_row if fixed_ref else jnp.max(st, axis=0, keepdims=True)))
            v_ext = jnp.concatenate([v_ref[pl.ds(key_start(rr)[2], NA_WIN), :], ones], axis=1)
            full = lax.dot_general(p.astype(jnp.bfloat16), v_ext, dn_first,
                                   preferred_element_type=jnp.float32)
            num = jnp.where(lane_q < HEAD_DIM, full[:GRID_W, :LANES], full[GRID_W:, :LANES])
            den = jnp.where(lane_q < HEAD_DIM, full[:GRID_W, LANES:], full[GRID_W:, LANES:])
            o_ref[rr * GRID_W:(rr + 1) * GRID_W, :] = (num / den).astype(o_ref.dtype)

    for fixed in (True, False):
        pl.when((ok_ref[step] == 1) if fixed else (ok_ref[step] != 1))(functools.partial(group, fixed))


def _nbr_ref_points(stats, rpb_l, batch, seq):
    ng = seq // NA_TOK
    norms = DA_NORM_MARGIN * jnp.sqrt(stats[:, 2:4, :NA_HEADS]).reshape(batch, ng, 2, NA_HEADS)
    kn, qn = norms[:, :, 0], norms[:, :, 1]
    pad = jnp.pad(kn, ((0, 0), (1, 1), (0, 0)))
    kn_win = jnp.maximum(jnp.maximum(pad[:, :-2], pad[:, 1:-1]), pad[:, 2:])
    bias = rpb_l.astype(jnp.float32) * LOG2E
    b_self = bias[:, NA_ROWS - 1, NA_COLS - 1]
    b_max = jnp.max(bias, axis=(1, 2))
    refpt = -(qn * kn_win) + b_self
    spread = 2.0 * qn * kn_win + (b_max - b_self)
    ok = jnp.all((spread <= NA_FIXED_REACH).reshape(batch, ng, NA_PAIRS, 2), axis=-1)
    refpt = jnp.transpose(refpt.reshape(batch, ng, NA_PAIRS, 2), (0, 2, 1, 3))
    return (jnp.transpose(ok, (0, 2, 1)).reshape(-1).astype(jnp.int32),
            refpt.reshape(-1).astype(jnp.float32))


def _nbr_attn(proj, stats, rpb_l, batch, seq):
    rows = seq // GRID_W
    ng = seq // NA_TOK
    m = batch * seq
    ok, refpt = _nbr_ref_points(stats, rpb_l, batch, seq)
    grid_spec = pltpu.PrefetchScalarGridSpec(
        num_scalar_prefetch=2,
        grid=(batch, NA_PAIRS, ng),
        in_specs=[
            pl.BlockSpec((None, NA_TOK, LANES), lambda b, hp, g, ok, rp: (CB_QB + hp, b * ng + g, 0)),
            pl.BlockSpec((None, seq, LANES), lambda b, hp, g, ok, rp: (CB_KB + hp, b, 0)),
            pl.BlockSpec((None, seq, LANES), lambda b, hp, g, ok, rp: (CB_VB + hp, b, 0)),
            pl.BlockSpec((None, NA_BIAS_ROWS, LANES), lambda b, hp, g, ok, rp: (hp, 0, 0)),
        ],
        out_specs=pl.BlockSpec((None, NA_TOK, LANES), lambda b, hp, g, ok, rp: (hp, b * ng + g, 0)),
    )
    return pl.pallas_call(
        functools.partial(_nbr_attn_kernel, rows=rows),
        grid_spec=grid_spec,
        out_shape=jax.ShapeDtypeStruct((NA_PAIRS, m, LANES), jnp.bfloat16),
        compiler_params=pltpu.CompilerParams(
            dimension_semantics=("arbitrary", "arbitrary", "arbitrary"),
            vmem_limit_bytes=VMEM_LIMIT),
        name="nbr_attn",
    )(ok, refpt, proj, proj, proj, _nbr_bias_table(rpb_l))


def _nbr_bias_table(rpb_l):
    col = jnp.arange(GRID_W)
    col_start = jnp.clip(col - NA_COLS // 2, 0, GRID_W - NA_COLS)
    col_in = (col[None, :] >= col_start[:, None]) & (col[None, :] < col_start[:, None] + NA_COLS)
    col_off = jnp.clip(col[None, :] - col[:, None] + NA_COLS - 1, 0, 2 * NA_COLS - 2)
    t = rpb_l.astype(jnp.float32)[:, :, col_off]
    t = jnp.where(col_in[None, None], t * LOG2E, NEG)
    t = t.reshape(NA_PAIRS, 2, 2 * NA_ROWS - 1, GRID_W, GRID_W)
    t = jnp.transpose(t, (0, 2, 4, 1, 3))
    return t.reshape(NA_PAIRS, NA_BIAS_ROWS, LANES)


MIX_TM = 512


def _cat_blocks(ref):
    return jnp.concatenate([ref[c] for c in range(ref.shape[0])], axis=-1)


def _mix_out_kernel(x_ref, ya_ref, yb_ref, ga_ref, gb_ref, wa_ref, wb_ref, wo_ref, nw_ref, o_ref):
    ya = _cat_blocks(ya_ref)
    yb = _cat_blocks(yb_ref)
    ga = _cat_blocks(ga_ref).astype(jnp.float32)
    gb = _cat_blocks(gb_ref).astype(jnp.float32)
    merged = (ga * jnp.dot(ya, wa_ref[...], preferred_element_type=jnp.float32)
              + gb * jnp.dot(yb, wb_ref[...], preferred_element_type=jnp.float32))
    t = jnp.dot(merged.astype(jnp.bfloat16), wo_ref[...], preferred_element_type=jnp.float32)
    o_ref[...] = x_ref[...] + _rms(t, nw_ref[...])


def _mix_out(x2, ya, yb, proj, wa, wb, wo, nw):
    m = x2.shape[0]
    const = lambda i: (0, 0)
    gate_blk = D_MODEL // LANES
    return pl.pallas_call(
        _mix_out_kernel,
        grid=(m // MIX_TM,),
        in_specs=[
            pl.BlockSpec((MIX_TM, D_MODEL), lambda i: (i, 0)),
            pl.BlockSpec((DA_HEADS, MIX_TM, LANES), lambda i: (0, i, 0)),
            pl.BlockSpec((NA_PAIRS, MIX_TM, LANES), lambda i: (0, i, 0)),
            pl.BlockSpec((gate_blk, MIX_TM, LANES), lambda i: (CB_GA // gate_blk, i, 0)),
            pl.BlockSpec((gate_blk, MIX_TM, LANES), lambda i: (CB_GB // gate_blk, i, 0)),
            pl.BlockSpec(wa.shape, const),
            pl.BlockSpec(wb.shape, const),
            pl.BlockSpec(wo.shape, const),
            pl.BlockSpec((1, D_MODEL), const),
        ],
        out_specs=pl.BlockSpec((MIX_TM, D_MODEL), lambda i: (i, 0)),
        out_shape=jax.ShapeDtypeStruct((m, D_MODEL), jnp.float32),
        compiler_params=pltpu.CompilerParams(
            dimension_semantics=("arbitrary",),
            vmem_limit_bytes=VMEM_LIMIT),
        name="mix_out",
    )(x2, ya, yb, proj, proj, wa, wb, wo, nw)


FFN_TM = 512
FFN_CHUNK = 256
FFN_NCHUNK = D_FF // FFN_CHUNK


def _ffn_kernel(x_ref, nw1_ref, wg_ref, wu_ref, wd_ref, nw2_ref, o_ref):
    x = x_ref[...]
    h = _rms(x, nw1_ref[...]).astype(jnp.bfloat16)
    f = jnp.zeros((FFN_TM, D_MODEL), jnp.float32)
    for c in range(FFN_NCHUNK):
        sl = slice(c * FFN_CHUNK, (c + 1) * FFN_CHUNK)
        g = jnp.dot(h, wg_ref[:, sl], preferred_element_type=jnp.float32)
        u = jnp.dot(h, wu_ref[:, sl], preferred_element_type=jnp.float32)
        a = (jax.nn.silu(g) * u).astype(jnp.bfloat16)
        f = f + jnp.dot(a, wd_ref[sl, :], preferred_element_type=jnp.float32)
    o_ref[...] = x + _rms(f, nw2_ref[...])


def _ffn(x1, nw1, wg, wu, wd, nw2):
    m = x1.shape[0]
    const = lambda i: (0, 0)
    once = pl.Buffered(1)
    return pl.pallas_call(
        _ffn_kernel,
        grid=(m // FFN_TM,),
        in_specs=[
            pl.BlockSpec((FFN_TM, D_MODEL), lambda i: (i, 0)),
            pl.BlockSpec((1, D_MODEL), const),
            pl.BlockSpec(wg.shape, const, pipeline_mode=once),
            pl.BlockSpec(wu.shape, const, pipeline_mode=once),
            pl.BlockSpec(wd.shape, const, pipeline_mode=once),
            pl.BlockSpec((1, D_MODEL), const),
        ],
        out_specs=pl.BlockSpec((FFN_TM, D_MODEL), lambda i: (i, 0)),
        out_shape=jax.ShapeDtypeStruct((m, D_MODEL), jnp.float32),
        compiler_params=pltpu.CompilerParams(
            dimension_semantics=("arbitrary",),
            vmem_limit_bytes=VMEM_LIMIT),
        name="ffn",
    )(x1, nw1, wg, wu, wd, nw2)


def kernel(x, pre_mix_w, w_in, b_gate, lambda_q1, lambda_k1, lambda_q2, lambda_k2, subln_w, rpb,
           w_branch_a, w_branch_b, w_out, post_mix_w, pre_ffn_w, w_gate, w_up, w_down, post_ffn_w):
    batch, seq, _ = x.shape
    depth = w_in.shape[0]
    assert depth == 1, "LAMBDA_INIT is specialised to a single layer"
    bf = jnp.bfloat16
    f32 = jnp.float32
    slopes = jnp.asarray([2.0 ** (-8.0 * (i + 1) / DA_HEADS) for i in range(DA_HEADS)], f32)
    grp = (jnp.arange(DA_W)[:, None] // HEAD_DIM == jnp.arange(LANES)[None, :]).astype(bf)
    x2 = x.reshape(batch * seq, D_MODEL)
    for l in range(depth):
        w_l = w_in[l].astype(bf)
        w_p = jnp.concatenate([w_l[:, :DA_W], w_l[:, 2 * DA_W:3 * DA_W],
                               w_l[:, DA_W:2 * DA_W], w_l[:, 3 * DA_W:]], axis=1)
        proj_t, proj_n, stats = _in_proj(x2, pre_mix_w[l].reshape(1, D_MODEL).astype(f32), w_p,
                                         b_gate[l].reshape(1, 2 * D_MODEL).astype(f32), grp)
        lam_vecs = jnp.stack([lambda_q1[l], lambda_k1[l], lambda_q2[l], lambda_k2[l]]).astype(f32)
        ya = _diff_attn(proj_t, proj_n, stats, slopes, lam_vecs,
                        subln_w[l].reshape(2 * HEAD_DIM, 1).astype(f32), batch, seq)
        yb = _nbr_attn(proj_n, stats, rpb[l], batch, seq)
        x1 = _mix_out(x2, ya, yb, proj_n, w_branch_a[l].astype(bf), w_branch_b[l].astype(bf),
                      w_out[l].astype(bf), post_mix_w[l].reshape(1, D_MODEL).astype(f32))
        x2 = _ffn(x1, pre_ffn_w[l].reshape(1, D_MODEL).astype(f32), w_gate[l].astype(bf),
                  w_up[l].astype(bf), w_down[l].astype(bf), post_ffn_w[l].reshape(1, D_MODEL).astype(f32))
    return x2.reshape(batch, seq, D_MODEL)
```

```python
import functools
import math

import jax
import jax.numpy as jnp
import numpy as np
from jax import lax
from jax.experimental import pallas as pl
from jax.experimental.pallas import tpu as pltpu

D_MODEL = 1024
HEAD_DIM = 64
DA_HEADS = 4
DA_W = DA_HEADS * 2 * HEAD_DIM
NA_HEADS = 8
NA_PAIRS = NA_HEADS // 2
NA_W = NA_HEADS * HEAD_DIM
GRID_W = 64
NA_ROWS = 8
NA_COLS = 16
D_FF = 2816
IN_COLS = 5120
EPS = 1e-6
NEG = -1e30
LANES = 128
LOG2E = math.log2(math.e)
LAMBDA_INIT = 0.8 - 0.6 * math.exp(-0.3 * 0)

CB_KA, CB_QB, CB_KB, CB_VB, CB_GA, CB_GB = 0, 4, 8, 12, 16, 24
N_COLBLK_N = 32
RB_QA, RB_VA = 0, 4

VMEM_LIMIT = 56 * 1024 * 1024


def _rms(xf, w):
    return xf * lax.rsqrt(jnp.mean(xf * xf, axis=-1, keepdims=True) + EPS) * w


IN_TM = 512
IN_TN = 1024
IN_NBLK = IN_TN // LANES
Q_SCALE = HEAD_DIM ** -0.5 * LOG2E
STAT_ROWS = 8


def _max_group_sqnorm(a, grp):
    ab = a.astype(jnp.bfloat16).astype(jnp.float32)
    n2 = jnp.dot((ab * ab).astype(jnp.bfloat16), grp, preferred_element_type=jnp.float32)
    return jnp.max(n2, axis=0, keepdims=True)


def _in_proj_kernel(x_ref, nw_ref, w_ref, b_ref, grp_ref, ot_ref, on_ref, st_ref):
    h = _rms(x_ref[...], nw_ref[...]).astype(jnp.bfloat16)

    def chunk(c):
        return jnp.dot(h, w_ref[:, c * IN_TN:(c + 1) * IN_TN], preferred_element_type=jnp.float32)

    def store_natural(c, val):
        for j in range(IN_NBLK):
            on_ref[(c - 1) * IN_NBLK + j] = val[:, j * LANES:(j + 1) * LANES].astype(on_ref.dtype)

    acc = chunk(0)
    qa = acc[:, :DA_W] * Q_SCALE
    for j in range(IN_NBLK // 2):
        ot_ref[RB_QA + j, 0] = qa[:, j * LANES:(j + 1) * LANES].T.astype(ot_ref.dtype)
        ot_ref[RB_VA + j, 0] = acc[:, DA_W + j * LANES:DA_W + (j + 1) * LANES].T.astype(ot_ref.dtype)
    q_stat = _max_group_sqnorm(qa, grp_ref[...])
    acc = chunk(1)
    k_stat = _max_group_sqnorm(acc[:, :DA_W], grp_ref[...])
    qb = acc[:, DA_W:] * Q_SCALE
    store_natural(1, jnp.concatenate([acc[:, :DA_W], qb], axis=1))
    qb_stat = _max_group_sqnorm(qb, grp_ref[...])
    acc = chunk(2)
    store_natural(2, acc)
    kb_stat = _max_group_sqnorm(acc[:, :NA_W], grp_ref[...])
    for c in (3, 4):
        gb = b_ref[:, (c - 3) * IN_TN:(c - 2) * IN_TN]
        store_natural(c, jax.nn.sigmoid(chunk(c) + gb))
    st_ref[0] = jnp.concatenate(
        [k_stat, q_stat, kb_stat, qb_stat, jnp.zeros((STAT_ROWS - 4, LANES), jnp.float32)], axis=0)


def _in_proj(x2, nw, w_bf, b_gate_row, grp):
    m = x2.shape[0]
    nt = m // IN_TM
    const = lambda i: (0, 0)
    return pl.pallas_call(
        _in_proj_kernel,
        grid=(nt,),
        in_specs=[
            pl.BlockSpec((IN_TM, D_MODEL), lambda i: (i, 0)),
            pl.BlockSpec((1, D_MODEL), const),
            pl.BlockSpec((D_MODEL, IN_COLS), const, pipeline_mode=pl.Buffered(1)),
            pl.BlockSpec((1, 2 * D_MODEL), const),
            pl.BlockSpec((DA_W, LANES), const),
        ],
        out_specs=[
            pl.BlockSpec((IN_NBLK, 1, LANES, IN_TM), lambda i: (0, i, 0, 0)),
            pl.BlockSpec((N_COLBLK_N, IN_TM, LANES), lambda i: (0, i, 0)),
            pl.BlockSpec((1, STAT_ROWS, LANES), lambda i: (i, 0, 0)),
        ],
        out_shape=[
            jax.ShapeDtypeStruct((IN_NBLK, nt, LANES, IN_TM), jnp.bfloat16),
            jax.ShapeDtypeStruct((N_COLBLK_N, m, LANES), jnp.bfloat16),
            jax.ShapeDtypeStruct((nt, STAT_ROWS, LANES), jnp.float32),
        ],
        compiler_params=pltpu.CompilerParams(
            dimension_semantics=("arbitrary",),
            vmem_limit_bytes=VMEM_LIMIT),
        name="in_proj",
    )(x2, nw, w_bf, b_gate_row, grp)


DA_TQ = 1024
DA_TK = 512
DA_QG = 256
DA_NG = DA_TQ // DA_QG
DA_QT = DA_TQ // IN_TM
DA_DIAG = DA_TQ // DA_TK
DA_NAUG = 9
BF16_EXACT_INT = 256
DA_AHEAD = 4
DA_SKIP_LOG2 = 127.0
DA_FIXED_REACH = 64.0
DA_NBOUND = 3
DA_NORM_MARGIN = 1.01


def _alibi_tables(slopes):
    f32, bf = np.float32, jnp.bfloat16
    s2 = slopes.astype(f32) * f32(LOG2E)
    c1 = s2.astype(bf)
    c2 = (s2 - c1.astype(f32)).astype(bf)
    c3 = (s2 - c1.astype(f32) - c2.astype(f32)).astype(bf)
    parts = np.stack([c1, c2, c3], axis=1).astype(f32)
    s2_used = parts[:, 0] + parts[:, 1] + parts[:, 2]
    dk = np.arange(DA_TK)
    dk_lo = (dk % BF16_EXACT_INT).astype(f32)
    dk_hi = (dk - dk % BF16_EXACT_INT).astype(f32)
    ones3 = np.ones((3,), f32)
    k_aug = np.concatenate([
        np.broadcast_to((dk_lo[:, None] * ones3)[None], (DA_HEADS, DA_TK, 3)),
        np.broadcast_to((dk_hi[:, None] * ones3)[None], (DA_HEADS, DA_TK, 3)),
        np.broadcast_to(parts[:, None, :], (DA_HEADS, DA_TK, 3)),
    ], axis=-1)
    pad_k = np.zeros((DA_HEADS, DA_TK, HEAD_DIM - DA_NAUG), f32)
    k_aug = np.concatenate([k_aug, pad_k], axis=-1)
    zeros_k = np.zeros((DA_HEADS, DA_TK, HEAD_DIM), f32)
    k_tab = np.stack([np.concatenate([zeros_k, k_aug], axis=-1),
                      np.concatenate([k_aug, zeros_k], axis=-1)], axis=1)
    dq = (np.arange(DA_TQ) % DA_QG).astype(f32)
    sig = np.asarray([1.0, -1.0], f32)
    q_par = sig[None, :, None, None] * parts[:, None, :, None] * np.ones((DA_TQ,), f32)
    q_off = -sig[None, :, None, None] * np.broadcast_to(dq, (DA_HEADS, 1, 3, DA_TQ))
    q_aug = np.concatenate([q_par, q_par, q_off], axis=2)
    q_aug = np.concatenate(
        [q_aug, np.zeros((DA_HEADS, 2, HEAD_DIM - DA_NAUG, DA_TQ), f32)], axis=2)
    return k_tab.astype(bf), q_aug.astype(bf), s2_used


def _diff_attn_kernel(s2_ref, bounds_ref, refpt_ref, q_ref, k_ref, v_ref, ktab_ref, qaug_ref, lam_ref, sw_ref, o_ref,
                      ka_ref, qm_ref, dist_ref, m_ref, l_ref, acc_ref, *, nkc):
    h = pl.program_id(1)
    qi = pl.program_id(2)
    slope2 = s2_ref[h]

    @pl.when(qi == 0)
    def _():
        lane = lax.broadcasted_iota(jnp.int32, (DA_TK, LANES), 1)
        for t in range(nkc):
            rows = slice(t * DA_TK, (t + 1) * DA_TK)
            kt = k_ref[rows, :]
            ka_ref[0, rows, :] = jnp.where(lane < HEAD_DIM, kt, ktab_ref[0])
            ka_ref[1, rows, :] = jnp.where(lane >= HEAD_DIM, kt, ktab_ref[1])

    for t in range(DA_QT):
        qt = q_ref[t]
        cols = slice(t * IN_TM, (t + 1) * IN_TM)
        for sg in range(2):
            qm_ref[sg, 0, :HEAD_DIM, cols] = qt[:HEAD_DIM]
            qm_ref[sg, 0, HEAD_DIM:, cols] = qaug_ref[sg, :, cols]
            qm_ref[sg, 1, :HEAD_DIM, cols] = qaug_ref[sg, :, cols]
            qm_ref[sg, 1, HEAD_DIM:, cols] = qt[HEAD_DIM:]
    dq = lax.broadcasted_iota(jnp.int32, (DA_TK, DA_QG), 1)
    dk = lax.broadcasted_iota(jnp.int32, (DA_TK, DA_QG), 0)
    dist_ref[...] = (dq - dk).astype(jnp.float32) * slope2
    m_ref[...] = jnp.full_like(m_ref, -jnp.inf)
    l_ref[...] = jnp.zeros_like(l_ref)
    acc_ref[...] = jnp.zeros_like(acc_ref)

    tiles = [(sub, g, mp) for sub in range(DA_DIAG) for g in range(DA_NG) for mp in range(2)]

    step = (pl.program_id(0) * DA_HEADS + h) * pl.num_programs(2) + qi

    def make_body(sg, diagonal, fixed_ref=False):
        def side(sub, g):
            if not diagonal:
                return sg, (1.0 if sg == 0 else -1.0), False
            c = g * DA_QG - sub * DA_TK
            if c >= DA_TK:
                return 0, 1.0, False
            if c <= -DA_QG:
                return 1, -1.0, False
            return 0, 1.0, True

        def body(kb, carry):
            def origin(sub, g):
                kc = kb * DA_DIAG + sub
                return (qi * DA_TQ + g * DA_QG - kc * DA_TK).astype(jnp.float32) * slope2

            corrections = {}

            def correction(sub, g):
                if (sub, g) not in corrections:
                    corrections[sub, g] = 2.0 * jnp.minimum(dist_ref[...] + origin(sub, g), 0.0)
                return corrections[sub, g]

            def scores(sub, g, mp):
                cols = slice(g * DA_QG, (g + 1) * DA_QG)
                k0 = pl.multiple_of((kb * DA_DIAG + sub) * DA_TK, DA_TK)
                variant, _, straddles = side(sub, g)
                st = jnp.dot(ka_ref[mp, pl.ds(k0, DA_TK), :], qm_ref[variant, mp, :, cols],
                             preferred_element_type=jnp.float32)
                return st + correction(sub, g) if straddles else st

            ahead = [scores(*tiles[i]) for i in range(DA_AHEAD)]
            for ti, (sub, g, mp) in enumerate(tiles):
                cols = slice(g * DA_QG, (g + 1) * DA_QG)
                st = ahead.pop(0)
                if ti + DA_AHEAD < len(tiles):
                    ahead.append(scores(*tiles[ti + DA_AHEAD]))
                shift = side(sub, g)[1] * origin(sub, g)
                vt = v_ref[kb * DA_DIAG + sub]
                if fixed_ref:
                    p = jnp.exp2(st - (refpt_ref[step * 2 + mp] + shift))
                    l_ref[mp, :, cols] = l_ref[mp, :, cols] + jnp.sum(p, axis=0, keepdims=True)
                    acc_ref[mp, :, cols] = acc_ref[mp, :, cols] + jnp.dot(
                        vt, p.astype(jnp.bfloat16), preferred_element_type=jnp.float32)
                else:
                    m_old = m_ref[mp, :, cols]
                    m_new = jnp.maximum(m_old, jnp.max(st, axis=0, keepdims=True) - shift)
                    alpha = jnp.exp2(m_old - m_new)
                    p = jnp.exp2(st - (m_new + shift))
                    l_ref[mp, :, cols] = alpha * l_ref[mp, :, cols] + jnp.sum(p, axis=0, keepdims=True)
                    acc_ref[mp, :, cols] = alpha * acc_ref[mp, :, cols] + jnp.dot(
                        vt, p.astype(jnp.bfloat16), preferred_element_type=jnp.float32)
                    m_ref[mp, :, cols] = m_new
            return carry

        return body

    lo = bounds_ref[step * DA_NBOUND]
    hi = bounds_ref[step * DA_NBOUND + 1]
    fixed_ok = bounds_ref[step * DA_NBOUND + 2]

    for fixed in (True, False):
        @pl.when((fixed_ok == 1) if fixed else (fixed_ok != 1))
        def _():
            make_body(0, True, fixed)(qi, 0)
            lax.fori_loop(lo, qi, make_body(0, False, fixed), 0)
            lax.fori_loop(qi + 1, hi, make_body(1, False, fixed), 0)

    lv = lam_ref[...]
    lam = (jnp.exp(jnp.sum(lv[0:1] * lv[1:2], axis=-1, keepdims=True))
           - jnp.exp(jnp.sum(lv[2:3] * lv[3:4], axis=-1, keepdims=True))
           + LAMBDA_INIT)
    o = acc_ref[0] / l_ref[0] - lam * (acc_ref[1] / l_ref[1])
    o = o * lax.rsqrt(jnp.mean(o * o, axis=0, keepdims=True) + EPS) * sw_ref[...] * (1.0 - LAMBDA_INIT)
    o_ref[...] = o.T.astype(o_ref.dtype)


def _skip_bounds(stats, s2_used, batch, seq):
    nq, nkc = seq // DA_TQ, seq // DA_TK
    nkb = nkc // DA_DIAG
    norms = DA_NORM_MARGIN * jnp.sqrt(stats[:, :2, :2 * DA_HEADS])
    norms = norms.reshape(batch, nkc, 2, DA_HEADS, 2)
    kn = norms[:, :, 0]
    qn = norms[:, :, 1].reshape(batch, nq, DA_QT, DA_HEADS, 2).max(axis=2)
    kn_self = kn.reshape(batch, nq, DA_DIAG, DA_HEADS, 2).max(axis=2)
    q_lo = jnp.arange(nq) * DA_TQ
    k_lo = jnp.arange(nkc) * DA_TK
    gap = jnp.maximum(jnp.maximum(q_lo[:, None] - (k_lo[None, :] + DA_TK - 1),
                                  k_lo[None, :] - (q_lo[:, None] + DA_TQ - 1)), 0)
    reach = qn[:, :, None] * (kn[:, None, :] + kn_self[:, :, None])
    bound = reach - s2_used[None, None, None, :, None] * gap[None, :, :, None, None].astype(jnp.float32)
    need = jnp.any(~(bound < -DA_SKIP_LOG2), axis=-1) | (gap == 0)[None, :, :, None]
    need = need.reshape(batch, nq, nkb, DA_DIAG, DA_HEADS).any(axis=3)
    blk = jnp.arange(nkb)[None, None, :, None]
    lo = jnp.min(jnp.where(need, blk, nkb), axis=2)
    hi = jnp.max(jnp.where(need, blk + 1, 0), axis=2)
    refpt = -(qn * kn_self)
    reach_blk = reach.reshape(batch, nq, nkb, DA_DIAG, DA_HEADS, 2).max(axis=(3, 5))
    visited = (blk >= lo[:, :, None]) & (blk < hi[:, :, None])
    fixed_ok = jnp.all(~visited | (reach_blk <= DA_FIXED_REACH), axis=2)
    bounds = jnp.stack([lo, hi, fixed_ok.astype(lo.dtype)], axis=-1)
    return (jnp.transpose(bounds, (0, 2, 1, 3)).reshape(-1).astype(jnp.int32),
            jnp.transpose(refpt, (0, 2, 1, 3)).reshape(-1).astype(jnp.float32))


def _diff_attn(proj_t, proj_n, stats, slopes, lam_vecs, subln_col, batch, seq):
    nq = seq // DA_TQ
    nkc = seq // DA_TK
    m = batch * seq
    k_tab, q_aug, s2_used = _alibi_tables(slopes)
    bounds, refpt = _skip_bounds(stats, s2_used, batch, seq)
    grid_spec = pltpu.PrefetchScalarGridSpec(
        num_scalar_prefetch=3,
        grid=(batch, DA_HEADS, nq),
        in_specs=[
            pl.BlockSpec((None, DA_QT, LANES, IN_TM), lambda b, h, qi, s, bd, rp: (RB_QA + h, b * nq + qi, 0, 0)),
            pl.BlockSpec((None, seq, LANES), lambda b, h, qi, s, bd, rp: (CB_KA + h, b, 0)),
            pl.BlockSpec((None, nkc, LANES, DA_TK), lambda b, h, qi, s, bd, rp: (RB_VA + h, b, 0, 0)),
            pl.BlockSpec((None, 2, DA_TK, LANES), lambda b, h, qi, s, bd, rp: (h, 0, 0, 0)),
            pl.BlockSpec((None, 2, HEAD_DIM, DA_TQ), lambda b, h, qi, s, bd, rp: (h, 0, 0, 0)),
            pl.BlockSpec((4, HEAD_DIM), lambda b, h, qi, s, bd, rp: (0, 0)),
            pl.BlockSpec((2 * HEAD_DIM, 1), lambda b, h, qi, s, bd, rp: (0, 0)),
        ],
        out_specs=pl.BlockSpec((None, DA_TQ, LANES), lambda b, h, qi, s, bd, rp: (h, b * nq + qi, 0)),
        scratch_shapes=[
            pltpu.VMEM((2, seq, LANES), jnp.bfloat16),
            pltpu.VMEM((2, 2, LANES, DA_TQ), jnp.bfloat16),
            pltpu.VMEM((DA_TK, DA_QG), jnp.float32),
            pltpu.VMEM((2, 1, DA_TQ), jnp.float32),
            pltpu.VMEM((2, 1, DA_TQ), jnp.float32),
            pltpu.VMEM((2, LANES, DA_TQ), jnp.float32),
        ],
    )
    return pl.pallas_call(
        functools.partial(_diff_attn_kernel, nkc=nkc),
        grid_spec=grid_spec,
        out_shape=jax.ShapeDtypeStruct((DA_HEADS, m, LANES), jnp.bfloat16),
        compiler_params=pltpu.CompilerParams(
            dimension_semantics=("arbitrary", "arbitrary", "arbitrary"),
            vmem_limit_bytes=VMEM_LIMIT),
        name="diff_attn",
    )(s2_used, bounds, refpt, proj_t, proj_n, proj_t, k_tab, q_aug, lam_vecs, subln_col)


NA_GROUP = 8
NA_TOK = NA_GROUP * GRID_W
NA_WIN = NA_ROWS * GRID_W
NA_FIXED_REACH = 64.0
NA_AHEAD = 2
NA_BIAS_ROWS = (2 * NA_ROWS - 1) * GRID_W


def _nbr_attn_kernel(ok_ref, refpt_ref, q_ref, k_ref, v_ref, bias_ref, o_ref, *, rows):
    g = pl.program_id(2)
    step = (pl.program_id(0) * NA_PAIRS + pl.program_id(1)) * pl.num_programs(2) + g
    lane_q = lax.broadcasted_iota(jnp.int32, (GRID_W, LANES), 1)
    dn_last = (((1,), (1,)), ((), ()))
    dn_first = (((0,), (0,)), ((), ()))
    ones = jnp.ones((NA_WIN, LANES), jnp.bfloat16)

    def key_start(rr):
        r = g * NA_GROUP + rr
        rs = jnp.clip(r - NA_ROWS // 2, 0, rows - NA_ROWS)
        return r, rs, pl.multiple_of(rs * GRID_W, GRID_W)

    def scores(rr):
        r, rs, k0 = key_start(rr)
        b0 = pl.multiple_of((NA_ROWS - 1 - (r - rs)) * GRID_W, GRID_W)
        q = q_ref[rr * GRID_W:(rr + 1) * GRID_W, :]
        zero = jnp.zeros_like(q)
        qbd = jnp.concatenate([jnp.where(lane_q < HEAD_DIM, q, zero),
                               jnp.where(lane_q >= HEAD_DIM, q, zero)], axis=0)
        return lax.dot_general(k_ref[pl.ds(k0, NA_WIN), :], qbd, dn_last,
                               preferred_element_type=jnp.float32) + bias_ref[pl.ds(b0, NA_WIN), :]

    def group(fixed_ref):
        if fixed_ref:
            lane_k = lax.broadcasted_iota(jnp.int32, (1, LANES), 1)
            ref_row = jnp.where(lane_k < HEAD_DIM, refpt_ref[step * 2], refpt_ref[step * 2 + 1])
        ahead = [scores(rr) for rr in range(NA_AHEAD)]
        for rr in range(NA_GROUP):
            st = ahead.pop(0)
            if rr + NA_AHEAD < NA_GROUP:
                ahead.append(scores(rr + NA_AHEAD))
            p = jnp.exp2(st - (ref_row if fixed_ref else jnp.max(st, axis=0, keepdims=True)))
            v_ext = jnp.concatenate([v_ref[pl.ds(key_start(rr)[2], NA_WIN), :], ones], axis=1)
            full = lax.dot_general(p.astype(jnp.bfloat16), v_ext, dn_first,
                                   preferred_element_type=jnp.float32)
            num = jnp.where(lane_q < HEAD_DIM, full[:GRID_W, :LANES], full[GRID_W:, :LANES])
            den = jnp.where(lane_q < HEAD_DIM, full[:GRID_W, LANES:], full[GRID_W:, LANES:])
            o_ref[rr * GRID_W:(rr + 1) * GRID_W, :] = (num / den).astype(o_ref.dtype)

    for fixed in (True, False):
        pl.when((ok_ref[step] == 1) if fixed else (ok_ref[step] != 1))(functools.partial(group, fixed))


def _nbr_ref_points(stats, rpb_l, batch, seq):
    ng = seq // NA_TOK
    norms = DA_NORM_MARGIN * jnp.sqrt(stats[:, 2:4, :NA_HEADS]).reshape(batch, ng, 2, NA_HEADS)
    kn, qn = norms[:, :, 0], norms[:, :, 1]
    pad = jnp.pad(kn, ((0, 0), (1, 1), (0, 0)))
    kn_win = jnp.maximum(jnp.maximum(pad[:, :-2], pad[:, 1:-1]), pad[:, 2:])
    bias = rpb_l.astype(jnp.float32) * LOG2E
    b_self = bias[:, NA_ROWS - 1, NA_COLS - 1]
    b_max = jnp.max(bias, axis=(1, 2))
    refpt = -(qn * kn_win) + b_self
    spread = 2.0 * qn * kn_win + (b_max - b_self)
    ok = jnp.all((spread <= NA_FIXED_REACH).reshape(batch, ng, NA_PAIRS, 2), axis=-1)
    refpt = jnp.transpose(refpt.reshape(batch, ng, NA_PAIRS, 2), (0, 2, 1, 3))
    return (jnp.transpose(ok, (0, 2, 1)).reshape(-1).astype(jnp.int32),
            refpt.reshape(-1).astype(jnp.float32))


def _nbr_attn(proj, stats, rpb_l, batch, seq):
    rows = seq // GRID_W
    ng = seq // NA_TOK
    m = batch * seq
    ok, refpt = _nbr_ref_points(stats, rpb_l, batch, seq)
    grid_spec = pltpu.PrefetchScalarGridSpec(
        num_scalar_prefetch=2,
        grid=(batch, NA_PAIRS, ng),
        in_specs=[
            pl.BlockSpec((None, NA_TOK, LANES), lambda b, hp, g, ok, rp: (CB_QB + hp, b * ng + g, 0)),
            pl.BlockSpec((None, seq, LANES), lambda b, hp, g, ok, rp: (CB_KB + hp, b, 0)),
            pl.BlockSpec((None, seq, LANES), lambda b, hp, g, ok, rp: (CB_VB + hp, b, 0)),
            pl.BlockSpec((None, NA_BIAS_ROWS, LANES), lambda b, hp, g, ok, rp: (hp, 0, 0)),
        ],
        out_specs=pl.BlockSpec((None, NA_TOK, LANES), lambda b, hp, g, ok, rp: (hp, b * ng + g, 0)),
    )
    return pl.pallas_call(
        functools.partial(_nbr_attn_kernel, rows=rows),
        grid_spec=grid_spec,
        out_shape=jax.ShapeDtypeStruct((NA_PAIRS, m, LANES), jnp.bfloat16),
        compiler_params=pltpu.CompilerParams(
            dimension_semantics=("arbitrary", "arbitrary", "arbitrary"),
            vmem_limit_bytes=VMEM_LIMIT),
        name="nbr_attn",
    )(ok, refpt, proj, proj, proj, _nbr_bias_table(rpb_l))


def _nbr_bias_table(rpb_l):
    col = np.arange(GRID_W)
    col_start = np.clip(col - NA_COLS // 2, 0, GRID_W - NA_COLS)
    col_in = (col[None, :] >= col_start[:, None]) & (col[None, :] < col_start[:, None] + NA_COLS)
    col_off = np.clip(col[None, :] - col[:, None] + NA_COLS - 1, 0, 2 * NA_COLS - 2)
    pick = (col_off[None] == np.arange(2 * NA_COLS - 1)[:, None, None]).astype(np.float32)
    t = jnp.einsum('hic,cqk->hiqk', rpb_l.astype(jnp.float32), pick,
                   precision=lax.Precision.HIGHEST)
    t = jnp.where(col_in[None, None], t * LOG2E, NEG)
    t = t.reshape(NA_PAIRS, 2, 2 * NA_ROWS - 1, GRID_W, GRID_W)
    t = jnp.transpose(t, (0, 2, 4, 1, 3))
    return t.reshape(NA_PAIRS, NA_BIAS_ROWS, LANES)


MIX_TM = 1024


def _cat_blocks(ref):
    return jnp.concatenate([ref[c] for c in range(ref.shape[0])], axis=-1)


def _mix_out_kernel(x_ref, ya_ref, yb_ref, ga_ref, gb_ref, wa_ref, wb_ref, wo_ref, nw_ref, o_ref):
    ya = _cat_blocks(ya_ref)
    yb = _cat_blocks(yb_ref)
    ga = _cat_blocks(ga_ref).astype(jnp.float32)
    gb = _cat_blocks(gb_ref).astype(jnp.float32)
    merged = (ga * jnp.dot(ya, wa_ref[...], preferred_element_type=jnp.float32)
              + gb * jnp.dot(yb, wb_ref[...], preferred_element_type=jnp.float32))
    t = jnp.dot(merged.astype(jnp.bfloat16), wo_ref[...], preferred_element_type=jnp.float32)
    o_ref[...] = x_ref[...] + _rms(t, nw_ref[...])


def _mix_out(x2, ya, yb, proj, wa, wb, wo, nw):
    m = x2.shape[0]
    const = lambda i: (0, 0)
    gate_blk = D_MODEL // LANES
    return pl.pallas_call(
        _mix_out_kernel,
        grid=(m // MIX_TM,),
        in_specs=[
            pl.BlockSpec((MIX_TM, D_MODEL), lambda i: (i, 0)),
            pl.BlockSpec((DA_HEADS, MIX_TM, LANES), lambda i: (0, i, 0)),
            pl.BlockSpec((NA_PAIRS, MIX_TM, LANES), lambda i: (0, i, 0)),
            pl.BlockSpec((gate_blk, MIX_TM, LANES), lambda i: (CB_GA // gate_blk, i, 0)),
            pl.BlockSpec((gate_blk, MIX_TM, LANES), lambda i: (CB_GB // gate_blk, i, 0)),
            pl.BlockSpec(wa.shape, const),
            pl.BlockSpec(wb.shape, const),
            pl.BlockSpec(wo.shape, const),
            pl.BlockSpec((1, D_MODEL), const),
        ],
        out_specs=pl.BlockSpec((MIX_TM, D_MODEL), lambda i: (i, 0)),
        out_shape=jax.ShapeDtypeStruct((m, D_MODEL), jnp.float32),
        compiler_params=pltpu.CompilerParams(
            dimension_semantics=("arbitrary",),
            vmem_limit_bytes=VMEM_LIMIT),
        name="mix_out",
    )(x2, ya, yb, proj, proj, wa, wb, wo, nw)


FFN_TM = 512
FFN_CHUNK = 256
FFN_NCHUNK = D_FF // FFN_CHUNK


def _ffn_kernel(x_ref, nw1_ref, wg_ref, wu_ref, wd_ref, nw2_ref, o_ref):
    x = x_ref[...]
    h = _rms(x, nw1_ref[...]).astype(jnp.bfloat16)
    f = jnp.zeros((FFN_TM, D_MODEL), jnp.float32)
    for c in range(FFN_NCHUNK):
        sl = slice(c * FFN_CHUNK, (c + 1) * FFN_CHUNK)
        g = jnp.dot(h, wg_ref[:, sl], preferred_element_type=jnp.float32)
        u = jnp.dot(h, wu_ref[:, sl], preferred_element_type=jnp.float32)
        a = (jax.nn.silu(g) * u).astype(jnp.bfloat16)
        f = f + jnp.dot(a, wd_ref[sl, :], preferred_element_type=jnp.float32)
    o_ref[...] = x + _rms(f, nw2_ref[...])


def _ffn(x1, nw1, wg, wu, wd, nw2):
    m = x1.shape[0]
    const = lambda i: (0, 0)
    once = pl.Buffered(1)
    return pl.pallas_call(
        _ffn_kernel,
        grid=(m // FFN_TM,),
        in_specs=[
            pl.BlockSpec((FFN_TM, D_MODEL), lambda i: (i, 0)),
            pl.BlockSpec((1, D_MODEL), const),
            pl.BlockSpec(wg.shape, const, pipeline_mode=once),
            pl.BlockSpec(wu.shape, const, pipeline_mode=once),
            pl.BlockSpec(wd.shape, const, pipeline_mode=once),
            pl.BlockSpec((1, D_MODEL), const),
        ],
        out_specs=pl.BlockSpec((FFN_TM, D_MODEL), lambda i: (i, 0)),
        out_shape=jax.ShapeDtypeStruct((m, D_MODEL), jnp.float32),
        compiler_params=pltpu.CompilerParams(
            dimension_semantics=("arbitrary",),
            vmem_limit_bytes=VMEM_LIMIT),
        name="ffn",
    )(x1, nw1, wg, wu, wd, nw2)


def kernel(x, pre_mix_w, w_in, b_gate, lambda_q1, lambda_k1, lambda_q2, lambda_k2, subln_w, rpb,
           w_branch_a, w_branch_b, w_out, post_mix_w, pre_ffn_w, w_gate, w_up, w_down, post_ffn_w):
    batch, seq, _ = x.shape
    depth = w_in.shape[0]
    assert depth == 1, "LAMBDA_INIT is specialised to a single layer"
    bf = jnp.bfloat16
    f32 = jnp.float32
    slopes = np.asarray([2.0 ** (-8.0 * (i + 1) / DA_HEADS) for i in range(DA_HEADS)], np.float32)
    grp = (jnp.arange(DA_W)[:, None] // HEAD_DIM == jnp.arange(LANES)[None, :]).astype(bf)
    x2 = x.reshape(batch * seq, D_MODEL)
    for l in range(depth):
        w_l = w_in[l].astype(bf)
        w_p = jnp.concatenate([w_l[:, :DA_W], w_l[:, 2 * DA_W:3 * DA_W],
                               w_l[:, DA_W:2 * DA_W], w_l[:, 3 * DA_W:]], axis=1)
        proj_t, proj_n, stats = _in_proj(x2, pre_mix_w[l].reshape(1, D_MODEL).astype(f32), w_p,
                                         b_gate[l].reshape(1, 2 * D_MODEL).astype(f32), grp)
        lam_vecs = jnp.stack([lambda_q1[l], lambda_k1[l], lambda_q2[l], lambda_k2[l]]).astype(f32)
        ya = _diff_attn(proj_t, proj_n, stats, slopes, lam_vecs,
                        subln_w[l].reshape(2 * HEAD_DIM, 1).astype(f32), batch, seq)
        yb = _nbr_attn(proj_n, stats, rpb[l], batch, seq)
        x1 = _mix_out(x2, ya, yb, proj_n, w_branch_a[l].astype(bf), w_branch_b[l].astype(bf),
                      w_out[l].astype(bf), post_mix_w[l].reshape(1, D_MODEL).astype(f32))
        x2 = _ffn(x1, pre_ffn_w[l].reshape(1, D_MODEL).astype(f32), w_gate[l].astype(bf),
                  w_up[l].astype(bf), w_down[l].astype(bf), post_ffn_w[l].reshape(1, D_MODEL).astype(f32))
    return x2.reshape(batch, seq, D_MODEL)
```

```python
import functools
import math

import jax
import jax.numpy as jnp
import numpy as np
from jax import lax
from jax.experimental import pallas as pl
from jax.experimental.pallas import tpu as pltpu

D_MODEL = 1024
HEAD_DIM = 64
DA_HEADS = 4
DA_W = DA_HEADS * 2 * HEAD_DIM
NA_HEADS = 8
NA_PAIRS = NA_HEADS // 2
NA_W = NA_HEADS * HEAD_DIM
GRID_W = 64
NA_ROWS = 8
NA_COLS = 16
D_FF = 2816
IN_COLS = 5120
EPS = 1e-6
NEG = -1e30
LANES = 128
LOG2E = math.log2(math.e)
LAMBDA_INIT = 0.8 - 0.6 * math.exp(-0.3 * 0)

CB_KA, CB_QB, CB_KB, CB_VB, CB_GA, CB_GB = 0, 4, 8, 12, 16, 24
N_COLBLK_N = 32
RB_QA, RB_VA = 0, 4

VMEM_LIMIT = 56 * 1024 * 1024


def _rms(xf, w):
    return xf * lax.rsqrt(jnp.mean(xf * xf, axis=-1, keepdims=True) + EPS) * w


IN_TM = 512
IN_TN = 1024
IN_NBLK = IN_TN // LANES
Q_SCALE = HEAD_DIM ** -0.5 * LOG2E
STAT_ROWS = 8


def _max_group_sqnorm(a, grp):
    ab = a.astype(jnp.bfloat16).astype(jnp.float32)
    n2 = jnp.dot((ab * ab).astype(jnp.bfloat16), grp, preferred_element_type=jnp.float32)
    return jnp.max(n2, axis=0, keepdims=True)


def _in_proj_kernel(x_ref, nw_ref, w_ref, b_ref, grp_ref, ot_ref, on_ref, st_ref):
    h = _rms(x_ref[...], nw_ref[...]).astype(jnp.bfloat16)

    def chunk(c):
        return jnp.dot(h, w_ref[:, c * IN_TN:(c + 1) * IN_TN], preferred_element_type=jnp.float32)

    def store_natural(c, val):
        for j in range(IN_NBLK):
            on_ref[(c - 1) * IN_NBLK + j] = val[:, j * LANES:(j + 1) * LANES].astype(on_ref.dtype)

    acc = chunk(0)
    qa = acc[:, :DA_W] * Q_SCALE
    for j in range(IN_NBLK // 2):
        ot_ref[RB_QA + j, 0] = qa[:, j * LANES:(j + 1) * LANES].T.astype(ot_ref.dtype)
        ot_ref[RB_VA + j, 0] = acc[:, DA_W + j * LANES:DA_W + (j + 1) * LANES].T.astype(ot_ref.dtype)
    q_stat = _max_group_sqnorm(qa, grp_ref[...])
    acc = chunk(1)
    k_stat = _max_group_sqnorm(acc[:, :DA_W], grp_ref[...])
    qb = acc[:, DA_W:] * Q_SCALE
    store_natural(1, jnp.concatenate([acc[:, :DA_W], qb], axis=1))
    qb_stat = _max_group_sqnorm(qb, grp_ref[...])
    acc = chunk(2)
    store_natural(2, acc)
    kb_stat = _max_group_sqnorm(acc[:, :NA_W], grp_ref[...])
    for c in (3, 4):
        gb = b_ref[:, (c - 3) * IN_TN:(c - 2) * IN_TN]
        store_natural(c, jax.nn.sigmoid(chunk(c) + gb))
    st_ref[0] = jnp.concatenate(
        [k_stat, q_stat, kb_stat, qb_stat, jnp.zeros((STAT_ROWS - 4, LANES), jnp.float32)], axis=0)


def _in_proj(x2, nw, w_bf, b_gate_row, grp):
    m = x2.shape[0]
    nt = m // IN_TM
    const = lambda i: (0, 0)
    return pl.pallas_call(
        _in_proj_kernel,
        grid=(nt,),
        in_specs=[
            pl.BlockSpec((IN_TM, D_MODEL), lambda i: (i, 0)),
            pl.BlockSpec((1, D_MODEL), const),
            pl.BlockSpec((D_MODEL, IN_COLS), const, pipeline_mode=pl.Buffered(1)),
            pl.BlockSpec((1, 2 * D_MODEL), const),
            pl.BlockSpec((DA_W, LANES), const),
        ],
        out_specs=[
            pl.BlockSpec((IN_NBLK, 1, LANES, IN_TM), lambda i: (0, i, 0, 0)),
            pl.BlockSpec((N_COLBLK_N, IN_TM, LANES), lambda i: (0, i, 0)),
            pl.BlockSpec((1, STAT_ROWS, LANES), lambda i: (i, 0, 0)),
        ],
        out_shape=[
            jax.ShapeDtypeStruct((IN_NBLK, nt, LANES, IN_TM), jnp.bfloat16),
            jax.ShapeDtypeStruct((N_COLBLK_N, m, LANES), jnp.bfloat16),
            jax.ShapeDtypeStruct((nt, STAT_ROWS, LANES), jnp.float32),
        ],
        compiler_params=pltpu.CompilerParams(
            dimension_semantics=("arbitrary",),
            vmem_limit_bytes=VMEM_LIMIT),
        name="in_proj",
    )(x2, nw, w_bf, b_gate_row, grp)


DA_TQ = 1024
DA_TK = 512
DA_QG = 256
DA_NG = DA_TQ // DA_QG
DA_QT = DA_TQ // IN_TM
DA_DIAG = DA_TQ // DA_TK
DA_NAUG = 9
BF16_EXACT_INT = 256
DA_AHEAD = 4
DA_SKIP_LOG2 = 127.0
DA_FIXED_REACH = 64.0
DA_NBOUND = 3
DA_NORM_MARGIN = 1.01


def _alibi_tables(slopes):
    f32, bf = np.float32, jnp.bfloat16
    s2 = slopes.astype(f32) * f32(LOG2E)
    c1 = s2.astype(bf)
    c2 = (s2 - c1.astype(f32)).astype(bf)
    c3 = (s2 - c1.astype(f32) - c2.astype(f32)).astype(bf)
    parts = np.stack([c1, c2, c3], axis=1).astype(f32)
    s2_used = parts[:, 0] + parts[:, 1] + parts[:, 2]
    dk = np.arange(DA_TK)
    dk_lo = (dk % BF16_EXACT_INT).astype(f32)
    dk_hi = (dk - dk % BF16_EXACT_INT).astype(f32)
    ones3 = np.ones((3,), f32)
    k_aug = np.concatenate([
        np.broadcast_to((dk_lo[:, None] * ones3)[None], (DA_HEADS, DA_TK, 3)),
        np.broadcast_to((dk_hi[:, None] * ones3)[None], (DA_HEADS, DA_TK, 3)),
        np.broadcast_to(parts[:, None, :], (DA_HEADS, DA_TK, 3)),
    ], axis=-1)
    pad_k = np.zeros((DA_HEADS, DA_TK, HEAD_DIM - DA_NAUG), f32)
    k_aug = np.concatenate([k_aug, pad_k], axis=-1)
    zeros_k = np.zeros((DA_HEADS, DA_TK, HEAD_DIM), f32)
    k_tab = np.stack([np.concatenate([zeros_k, k_aug], axis=-1),
                      np.concatenate([k_aug, zeros_k], axis=-1)], axis=1)
    dq = (np.arange(DA_TQ) % DA_QG).astype(f32)
    sig = np.asarray([1.0, -1.0], f32)
    q_par = sig[None, :, None, None] * parts[:, None, :, None] * np.ones((DA_TQ,), f32)
    q_off = -sig[None, :, None, None] * np.broadcast_to(dq, (DA_HEADS, 1, 3, DA_TQ))
    q_aug = np.concatenate([q_par, q_par, q_off], axis=2)
    q_aug = np.concatenate(
        [q_aug, np.zeros((DA_HEADS, 2, HEAD_DIM - DA_NAUG, DA_TQ), f32)], axis=2)
    return k_tab.astype(bf), q_aug.astype(bf), s2_used


def _diff_attn_kernel(s2_ref, bounds_ref, refpt_ref, q_ref, k_ref, v_ref, ktab_ref, qaug_ref, lam_ref, sw_ref, o_ref,
                      ka_ref, qm_ref, dist_ref, m_ref, l_ref, acc_ref, *, nkc):
    h = pl.program_id(1)
    qi = pl.program_id(2)
    slope2 = s2_ref[h]

    @pl.when(qi == 0)
    def _():
        lane = lax.broadcasted_iota(jnp.int32, (DA_TK, LANES), 1)
        for t in range(nkc):
            rows = slice(t * DA_TK, (t + 1) * DA_TK)
            kt = k_ref[rows, :]
            ka_ref[0, rows, :] = jnp.where(lane < HEAD_DIM, kt, ktab_ref[0])
            ka_ref[1, rows, :] = jnp.where(lane >= HEAD_DIM, kt, ktab_ref[1])

    for t in range(DA_QT):
        qt = q_ref[t]
        cols = slice(t * IN_TM, (t + 1) * IN_TM)
        for sg in range(2):
            qm_ref[sg, 0, :HEAD_DIM, cols] = qt[:HEAD_DIM]
            qm_ref[sg, 0, HEAD_DIM:, cols] = qaug_ref[sg, :, cols]
            qm_ref[sg, 1, :HEAD_DIM, cols] = qaug_ref[sg, :, cols]
            qm_ref[sg, 1, HEAD_DIM:, cols] = qt[HEAD_DIM:]
    dq = lax.broadcasted_iota(jnp.int32, (DA_TK, DA_QG), 1)
    dk = lax.broadcasted_iota(jnp.int32, (DA_TK, DA_QG), 0)
    dist_ref[...] = (dq - dk).astype(jnp.float32) * slope2
    m_ref[...] = jnp.full_like(m_ref, -jnp.inf)
    l_ref[...] = jnp.zeros_like(l_ref)
    acc_ref[...] = jnp.zeros_like(acc_ref)

    tiles = [(sub, g, mp) for sub in range(DA_DIAG) for g in range(DA_NG) for mp in range(2)]

    step = (pl.program_id(0) * DA_HEADS + h) * pl.num_programs(2) + qi

    def make_body(sg, diagonal, fixed_ref=False):
        def side(sub, g):
            if not diagonal:
                return sg, (1.0 if sg == 0 else -1.0), False
            c = g * DA_QG - sub * DA_TK
            if c >= DA_TK:
                return 0, 1.0, False
            if c <= -DA_QG:
                return 1, -1.0, False
            return 0, 1.0, True

        def body(kb, carry):
            def origin(sub, g):
                kc = kb * DA_DIAG + sub
                return (qi * DA_TQ + g * DA_QG - kc * DA_TK).astype(jnp.float32) * slope2

            corrections = {}

            def correction(sub, g):
                if (sub, g) not in corrections:
                    corrections[sub, g] = 2.0 * jnp.minimum(dist_ref[...] + origin(sub, g), 0.0)
                return corrections[sub, g]

            def scores(sub, g, mp):
                cols = slice(g * DA_QG, (g + 1) * DA_QG)
                k0 = pl.multiple_of((kb * DA_DIAG + sub) * DA_TK, DA_TK)
                variant, _, straddles = side(sub, g)
                st = jnp.dot(ka_ref[mp, pl.ds(k0, DA_TK), :], qm_ref[variant, mp, :, cols],
                             preferred_element_type=jnp.float32)
                return st + correction(sub, g) if straddles else st

            ahead = [scores(*tiles[i]) for i in range(DA_AHEAD)]
            for ti, (sub, g, mp) in enumerate(tiles):
                cols = slice(g * DA_QG, (g + 1) * DA_QG)
                st = ahead.pop(0)
                if ti + DA_AHEAD < len(tiles):
                    ahead.append(scores(*tiles[ti + DA_AHEAD]))
                shift = side(sub, g)[1] * origin(sub, g)
                vt = v_ref[kb * DA_DIAG + sub]
                if fixed_ref:
                    p = jnp.exp2(st - (refpt_ref[step * 2 + mp] + shift))
                    l_ref[mp, :, cols] = l_ref[mp, :, cols] + jnp.sum(p, axis=0, keepdims=True)
                    acc_ref[mp, :, cols] = acc_ref[mp, :, cols] + jnp.dot(
                        vt, p.astype(jnp.bfloat16), preferred_element_type=jnp.float32)
                else:
                    m_old = m_ref[mp, :, cols]
                    m_new = jnp.maximum(m_old, jnp.max(st, axis=0, keepdims=True) - shift)
                    alpha = jnp.exp2(m_old - m_new)
                    p = jnp.exp2(st - (m_new + shift))
                    l_ref[mp, :, cols] = alpha * l_ref[mp, :, cols] + jnp.sum(p, axis=0, keepdims=True)
                    acc_ref[mp, :, cols] = alpha * acc_ref[mp, :, cols] + jnp.dot(
                        vt, p.astype(jnp.bfloat16), preferred_element_type=jnp.float32)
                    m_ref[mp, :, cols] = m_new
            return carry

        return body

    lo = bounds_ref[step * DA_NBOUND]
    hi = bounds_ref[step * DA_NBOUND + 1]
    fixed_ok = bounds_ref[step * DA_NBOUND + 2]

    for fixed in (True, False):
        @pl.when((fixed_ok == 1) if fixed else (fixed_ok != 1))
        def _():
            make_body(0, True, fixed)(qi, 0)
            lax.fori_loop(lo, qi, make_body(0, False, fixed), 0)
            lax.fori_loop(qi + 1, hi, make_body(1, False, fixed), 0)

    lv = lam_ref[...]
    lam = (jnp.exp(jnp.sum(lv[0:1] * lv[1:2], axis=-1, keepdims=True))
           - jnp.exp(jnp.sum(lv[2:3] * lv[3:4], axis=-1, keepdims=True))
           + LAMBDA_INIT)
    o = acc_ref[0] / l_ref[0] - lam * (acc_ref[1] / l_ref[1])
    o = o * lax.rsqrt(jnp.mean(o * o, axis=0, keepdims=True) + EPS) * sw_ref[...] * (1.0 - LAMBDA_INIT)
    o_ref[...] = o.T.astype(o_ref.dtype)


def _skip_bounds(stats, s2_used, batch, seq):
    nq, nkc = seq // DA_TQ, seq // DA_TK
    nkb = nkc // DA_DIAG
    norms = DA_NORM_MARGIN * jnp.sqrt(stats[:, :2, :2 * DA_HEADS])
    norms = norms.reshape(batch, nkc, 2, DA_HEADS, 2)
    kn = norms[:, :, 0]
    qn = norms[:, :, 1].reshape(batch, nq, DA_QT, DA_HEADS, 2).max(axis=2)
    kn_self = kn.reshape(batch, nq, DA_DIAG, DA_HEADS, 2).max(axis=2)
    q_lo = jnp.arange(nq) * DA_TQ
    k_lo = jnp.arange(nkc) * DA_TK
    gap = jnp.maximum(jnp.maximum(q_lo[:, None] - (k_lo[None, :] + DA_TK - 1),
                                  k_lo[None, :] - (q_lo[:, None] + DA_TQ - 1)), 0)
    reach = qn[:, :, None] * (kn[:, None, :] + kn_self[:, :, None])
    bound = reach - s2_used[None, None, None, :, None] * gap[None, :, :, None, None].astype(jnp.float32)
    need = jnp.any(~(bound < -DA_SKIP_LOG2), axis=-1) | (gap == 0)[None, :, :, None]
    need = need.reshape(batch, nq, nkb, DA_DIAG, DA_HEADS).any(axis=3)
    blk = jnp.arange(nkb)[None, None, :, None]
    lo = jnp.min(jnp.where(need, blk, nkb), axis=2)
    hi = jnp.max(jnp.where(need, blk + 1, 0), axis=2)
    refpt = -(qn * kn_self)
    reach_blk = reach.reshape(batch, nq, nkb, DA_DIAG, DA_HEADS, 2).max(axis=(3, 5))
    visited = (blk >= lo[:, :, None]) & (blk < hi[:, :, None])
    fixed_ok = jnp.all(~visited | (reach_blk <= DA_FIXED_REACH), axis=2)
    bounds = jnp.stack([lo, hi, fixed_ok.astype(lo.dtype)], axis=-1)
    return (jnp.transpose(bounds, (0, 2, 1, 3)).reshape(-1).astype(jnp.int32),
            jnp.transpose(refpt, (0, 2, 1, 3)).reshape(-1).astype(jnp.float32))


def _diff_attn(proj_t, proj_n, stats, slopes, lam_vecs, subln_col, batch, seq):
    nq = seq // DA_TQ
    nkc = seq // DA_TK
    m = batch * seq
    k_tab, q_aug, s2_used = _alibi_tables(slopes)
    bounds, refpt = _skip_bounds(stats, s2_used, batch, seq)
    grid_spec = pltpu.PrefetchScalarGridSpec(
        num_scalar_prefetch=3,
        grid=(batch, DA_HEADS, nq),
        in_specs=[
            pl.BlockSpec((None, DA_QT, LANES, IN_TM), lambda b, h, qi, s, bd, rp: (RB_QA + h, b * nq + qi, 0, 0)),
            pl.BlockSpec((None, seq, LANES), lambda b, h, qi, s, bd, rp: (CB_KA + h, b, 0)),
            pl.BlockSpec((None, nkc, LANES, DA_TK), lambda b, h, qi, s, bd, rp: (RB_VA + h, b, 0, 0)),
            pl.BlockSpec((None, 2, DA_TK, LANES), lambda b, h, qi, s, bd, rp: (h, 0, 0, 0)),
            pl.BlockSpec((None, 2, HEAD_DIM, DA_TQ), lambda b, h, qi, s, bd, rp: (h, 0, 0, 0)),
            pl.BlockSpec((4, HEAD_DIM), lambda b, h, qi, s, bd, rp: (0, 0)),
            pl.BlockSpec((2 * HEAD_DIM, 1), lambda b, h, qi, s, bd, rp: (0, 0)),
        ],
        out_specs=pl.BlockSpec((None, DA_TQ, LANES), lambda b, h, qi, s, bd, rp: (h, b * nq + qi, 0)),
        scratch_shapes=[
            pltpu.VMEM((2, seq, LANES), jnp.bfloat16),
            pltpu.VMEM((2, 2, LANES, DA_TQ), jnp.bfloat16),
            pltpu.VMEM((DA_TK, DA_QG), jnp.float32),
            pltpu.VMEM((2, 1, DA_TQ), jnp.float32),
            pltpu.VMEM((2, 1, DA_TQ), jnp.float32),
            pltpu.VMEM((2, LANES, DA_TQ), jnp.float32),
        ],
    )
    return pl.pallas_call(
        functools.partial(_diff_attn_kernel, nkc=nkc),
        grid_spec=grid_spec,
        out_shape=jax.ShapeDtypeStruct((DA_HEADS, m, LANES), jnp.bfloat16),
        compiler_params=pltpu.CompilerParams(
            dimension_semantics=("arbitrary", "arbitrary", "arbitrary"),
            vmem_limit_bytes=VMEM_LIMIT),
        name="diff_attn",
    )(s2_used, bounds, refpt, proj_t, proj_n, proj_t, k_tab, q_aug, lam_vecs, subln_col)


NA_GROUP = 32
NA_TOK = NA_GROUP * GRID_W
NA_WIN = NA_ROWS * GRID_W
NA_FIXED_REACH = 64.0
NA_AHEAD = 2
NA_BIAS_ROWS = (2 * NA_ROWS - 1) * GRID_W


def _nbr_attn_kernel(ok_ref, refpt_ref, q_ref, k_ref, v_ref, bias_ref, o_ref, *, rows):
    g = pl.program_id(2)
    step = (pl.program_id(0) * NA_PAIRS + pl.program_id(1)) * pl.num_programs(2) + g
    lane_q = lax.broadcasted_iota(jnp.int32, (GRID_W, LANES), 1)
    dn_last = (((1,), (1,)), ((), ()))
    dn_first = (((0,), (0,)), ((), ()))
    ones = jnp.ones((NA_WIN, LANES), jnp.bfloat16)

    def key_start(rr):
        r = g * NA_GROUP + rr
        rs = jnp.clip(r - NA_ROWS // 2, 0, rows - NA_ROWS)
        return r, rs, pl.multiple_of(rs * GRID_W, GRID_W)

    def scores(rr):
        r, rs, k0 = key_start(rr)
        b0 = pl.multiple_of((NA_ROWS - 1 - (r - rs)) * GRID_W, GRID_W)
        q = q_ref[rr * GRID_W:(rr + 1) * GRID_W, :]
        zero = jnp.zeros_like(q)
        qbd = jnp.concatenate([jnp.where(lane_q < HEAD_DIM, q, zero),
                               jnp.where(lane_q >= HEAD_DIM, q, zero)], axis=0)
        return lax.dot_general(k_ref[pl.ds(k0, NA_WIN), :], qbd, dn_last,
                               preferred_element_type=jnp.float32) + bias_ref[pl.ds(b0, NA_WIN), :]

    def group(fixed_ref):
        if fixed_ref:
            lane_k = lax.broadcasted_iota(jnp.int32, (1, LANES), 1)
            ref_row = jnp.where(lane_k < HEAD_DIM, refpt_ref[step * 2], refpt_ref[step * 2 + 1])
        ahead = [scores(rr) for rr in range(NA_AHEAD)]
        for rr in range(NA_GROUP):
            st = ahead.pop(0)
            if rr + NA_AHEAD < NA_GROUP:
                ahead.append(scores(rr + NA_AHEAD))
            p = jnp.exp2(st - (ref_row if fixed_ref else jnp.max(st, axis=0, keepdims=True)))
            v_ext = jnp.concatenate([v_ref[pl.ds(key_start(rr)[2], NA_WIN), :], ones], axis=1)
            full = lax.dot_general(p.astype(jnp.bfloat16), v_ext, dn_first,
                                   preferred_element_type=jnp.float32)
            num = jnp.where(lane_q < HEAD_DIM, full[:GRID_W, :LANES], full[GRID_W:, :LANES])
            den = jnp.where(lane_q < HEAD_DIM, full[:GRID_W, LANES:], full[GRID_W:, LANES:])
            o_ref[rr * GRID_W:(rr + 1) * GRID_W, :] = (num / den).astype(o_ref.dtype)

    for fixed in (True, False):
        pl.when((ok_ref[step] == 1) if fixed else (ok_ref[step] != 1))(functools.partial(group, fixed))


def _nbr_ref_points(stats, rpb_l, batch, seq):
    ng = seq // NA_TOK
    nt = seq // IN_TM
    assert NA_ROWS * GRID_W <= IN_TM, "a query's window must stay inside its own and the adjacent tiles"
    norms = DA_NORM_MARGIN * jnp.sqrt(stats[:, 2:4, :NA_HEADS]).reshape(batch, nt, 2, NA_HEADS)
    kn, qn = norms[:, :, 0], norms[:, :, 1]
    pad = jnp.pad(kn, ((0, 0), (1, 1), (0, 0)))
    kn_win = jnp.maximum(jnp.maximum(pad[:, :-2], pad[:, 1:-1]), pad[:, 2:])
    kn_win = kn_win.reshape(batch, ng, nt // ng, NA_HEADS).max(axis=2)
    qn = qn.reshape(batch, ng, nt // ng, NA_HEADS).max(axis=2)
    bias = rpb_l.astype(jnp.float32) * LOG2E
    b_self = bias[:, NA_ROWS - 1, NA_COLS - 1]
    b_max = jnp.max(bias, axis=(1, 2))
    refpt = -(qn * kn_win) + b_self
    spread = 2.0 * qn * kn_win + (b_max - b_self)
    ok = jnp.all((spread <= NA_FIXED_REACH).reshape(batch, ng, NA_PAIRS, 2), axis=-1)
    refpt = jnp.transpose(refpt.reshape(batch, ng, NA_PAIRS, 2), (0, 2, 1, 3))
    return (jnp.transpose(ok, (0, 2, 1)).reshape(-1).astype(jnp.int32),
            refpt.reshape(-1).astype(jnp.float32))


def _nbr_attn(proj, stats, rpb_l, batch, seq):
    rows = seq // GRID_W
    ng = seq // NA_TOK
    m = batch * seq
    ok, refpt = _nbr_ref_points(stats, rpb_l, batch, seq)
    grid_spec = pltpu.PrefetchScalarGridSpec(
        num_scalar_prefetch=2,
        grid=(batch, NA_PAIRS, ng),
        in_specs=[
            pl.BlockSpec((None, NA_TOK, LANES), lambda b, hp, g, ok, rp: (CB_QB + hp, b * ng + g, 0)),
            pl.BlockSpec((None, seq, LANES), lambda b, hp, g, ok, rp: (CB_KB + hp, b, 0)),
            pl.BlockSpec((None, seq, LANES), lambda b, hp, g, ok, rp: (CB_VB + hp, b, 0)),
            pl.BlockSpec((None, NA_BIAS_ROWS, LANES), lambda b, hp, g, ok, rp: (hp, 0, 0)),
        ],
        out_specs=pl.BlockSpec((None, NA_TOK, LANES), lambda b, hp, g, ok, rp: (hp, b * ng + g, 0)),
    )
    return pl.pallas_call(
        functools.partial(_nbr_attn_kernel, rows=rows),
        grid_spec=grid_spec,
        out_shape=jax.ShapeDtypeStruct((NA_PAIRS, m, LANES), jnp.bfloat16),
        compiler_params=pltpu.CompilerParams(
            dimension_semantics=("arbitrary", "arbitrary", "arbitrary"),
            vmem_limit_bytes=VMEM_LIMIT),
        name="nbr_attn",
    )(ok, refpt, proj, proj, proj, _nbr_bias_table(rpb_l))


def _nbr_bias_table(rpb_l):
    col = np.arange(GRID_W)
    col_start = np.clip(col - NA_COLS // 2, 0, GRID_W - NA_COLS)
    col_in = (col[None, :] >= col_start[:, None]) & (col[None, :] < col_start[:, None] + NA_COLS)
    col_off = np.clip(col[None, :] - col[:, None] + NA_COLS - 1, 0, 2 * NA_COLS - 2)
    pick = (col_off[None] == np.arange(2 * NA_COLS - 1)[:, None, None]).astype(np.float32)
    t = jnp.einsum('hic,cqk->hiqk', rpb_l.astype(jnp.float32), pick,
                   precision=lax.Precision.HIGHEST)
    t = jnp.where(col_in[None, None], t * LOG2E, NEG)
    t = t.reshape(NA_PAIRS, 2, 2 * NA_ROWS - 1, GRID_W, GRID_W)
    t = jnp.transpose(t, (0, 2, 4, 1, 3))
    return t.reshape(NA_PAIRS, NA_BIAS_ROWS, LANES)


MIX_TM = 1024


def _cat_blocks(ref):
    return jnp.concatenate([ref[c] for c in range(ref.shape[0])], axis=-1)


def _mix_out_kernel(x_ref, ya_ref, yb_ref, ga_ref, gb_ref, wa_ref, wb_ref, wo_ref, nw_ref, o_ref):
    ya = _cat_blocks(ya_ref)
    yb = _cat_blocks(yb_ref)
    ga = _cat_blocks(ga_ref).astype(jnp.float32)
    gb = _cat_blocks(gb_ref).astype(jnp.float32)
    merged = (ga * jnp.dot(ya, wa_ref[...], preferred_element_type=jnp.float32)
              + gb * jnp.dot(yb, wb_ref[...], preferred_element_type=jnp.float32))
    t = jnp.dot(merged.astype(jnp.bfloat16), wo_ref[...], preferred_element_type=jnp.float32)
    o_ref[...] = x_ref[...] + _rms(t, nw_ref[...])


def _mix_out(x2, ya, yb, proj, wa, wb, wo, nw):
    m = x2.shape[0]
    const = lambda i: (0, 0)
    gate_blk = D_MODEL // LANES
    return pl.pallas_call(
        _mix_out_kernel,
        grid=(m // MIX_TM,),
        in_specs=[
            pl.BlockSpec((MIX_TM, D_MODEL), lambda i: (i, 0)),
            pl.BlockSpec((DA_HEADS, MIX_TM, LANES), lambda i: (0, i, 0)),
            pl.BlockSpec((NA_PAIRS, MIX_TM, LANES), lambda i: (0, i, 0)),
            pl.BlockSpec((gate_blk, MIX_TM, LANES), lambda i: (CB_GA // gate_blk, i, 0)),
            pl.BlockSpec((gate_blk, MIX_TM, LANES), lambda i: (CB_GB // gate_blk, i, 0)),
            pl.BlockSpec(wa.shape, const),
            pl.BlockSpec(wb.shape, const),
            pl.BlockSpec(wo.shape, const),
            pl.BlockSpec((1, D_MODEL), const),
        ],
        out_specs=pl.BlockSpec((MIX_TM, D_MODEL), lambda i: (i, 0)),
        out_shape=jax.ShapeDtypeStruct((m, D_MODEL), jnp.float32),
        compiler_params=pltpu.CompilerParams(
            dimension_semantics=("arbitrary",),
            vmem_limit_bytes=VMEM_LIMIT),
        name="mix_out",
    )(x2, ya, yb, proj, proj, wa, wb, wo, nw)


FFN_TM = 512
FFN_CHUNK = 256
FFN_NCHUNK = D_FF // FFN_CHUNK


def _ffn_kernel(x_ref, nw1_ref, wg_ref, wu_ref, wd_ref, nw2_ref, o_ref):
    x = x_ref[...]
    h = _rms(x, nw1_ref[...]).astype(jnp.bfloat16)
    f = jnp.zeros((FFN_TM, D_MODEL), jnp.float32)
    for c in range(FFN_NCHUNK):
        sl = slice(c * FFN_CHUNK, (c + 1) * FFN_CHUNK)
        g = jnp.dot(h, wg_ref[:, sl], preferred_element_type=jnp.float32)
        u = jnp.dot(h, wu_ref[:, sl], preferred_element_type=jnp.float32)
        a = (jax.nn.silu(g) * u).astype(jnp.bfloat16)
        f = f + jnp.dot(a, wd_ref[sl, :], preferred_element_type=jnp.float32)
    o_ref[...] = x + _rms(f, nw2_ref[...])


def _ffn(x1, nw1, wg, wu, wd, nw2):
    m = x1.shape[0]
    const = lambda i: (0, 0)
    once = pl.Buffered(1)
    return pl.pallas_call(
        _ffn_kernel,
        grid=(m // FFN_TM,),
        in_specs=[
            pl.BlockSpec((FFN_TM, D_MODEL), lambda i: (i, 0)),
            pl.BlockSpec((1, D_MODEL), const),
            pl.BlockSpec(wg.shape, const, pipeline_mode=once),
            pl.BlockSpec(wu.shape, const, pipeline_mode=once),
            pl.BlockSpec(wd.shape, const, pipeline_mode=once),
            pl.BlockSpec((1, D_MODEL), const),
        ],
        out_specs=pl.BlockSpec((FFN_TM, D_MODEL), lambda i: (i, 0)),
        out_shape=jax.ShapeDtypeStruct((m, D_MODEL), jnp.float32),
        compiler_params=pltpu.CompilerParams(
            dimension_semantics=("arbitrary",),
            vmem_limit_bytes=VMEM_LIMIT),
        name="ffn",
    )(x1, nw1, wg, wu, wd, nw2)


def kernel(x, pre_mix_w, w_in, b_gate, lambda_q1, lambda_k1, lambda_q2, lambda_k2, subln_w, rpb,
           w_branch_a, w_branch_b, w_out, post_mix_w, pre_ffn_w, w_gate, w_up, w_down, post_ffn_w):
    batch, seq, _ = x.shape
    depth = w_in.shape[0]
    assert depth == 1, "LAMBDA_INIT is specialised to a single layer"
    bf = jnp.bfloat16
    f32 = jnp.float32
    slopes = np.asarray([2.0 ** (-8.0 * (i + 1) / DA_HEADS) for i in range(DA_HEADS)], np.float32)
    grp = (jnp.arange(DA_W)[:, None] // HEAD_DIM == jnp.arange(LANES)[None, :]).astype(bf)
    x2 = x.reshape(batch * seq, D_MODEL)
    for l in range(depth):
        w_l = w_in[l].astype(bf)
        w_p = jnp.concatenate([w_l[:, :DA_W], w_l[:, 2 * DA_W:3 * DA_W],
                               w_l[:, DA_W:2 * DA_W], w_l[:, 3 * DA_W:]], axis=1)
        proj_t, proj_n, stats = _in_proj(x2, pre_mix_w[l].reshape(1, D_MODEL).astype(f32), w_p,
                                         b_gate[l].reshape(1, 2 * D_MODEL).astype(f32), grp)
        lam_vecs = jnp.stack([lambda_q1[l], lambda_k1[l], lambda_q2[l], lambda_k2[l]]).astype(f32)
        ya = _diff_attn(proj_t, proj_n, stats, slopes, lam_vecs,
                        subln_w[l].reshape(2 * HEAD_DIM, 1).astype(f32), batch, seq)
        yb = _nbr_attn(proj_n, stats, rpb[l], batch, seq)
        x1 = _mix_out(x2, ya, yb, proj_n, w_branch_a[l].astype(bf), w_branch_b[l].astype(bf),
                      w_out[l].astype(bf), post_mix_w[l].reshape(1, D_MODEL).astype(f32))
        x2 = _ffn(x1, pre_ffn_w[l].reshape(1, D_MODEL).astype(f32), w_gate[l].astype(bf),
                  w_up[l].astype(bf), w_down[l].astype(bf), post_ffn_w[l].reshape(1, D_MODEL).astype(f32))
    return x2.reshape(batch, seq, D_MODEL)
```

```python
import functools
import math

import jax
import jax.numpy as jnp
import numpy as np
from jax import lax
from jax.experimental import pallas as pl
from jax.experimental.pallas import tpu as pltpu

D_MODEL = 1024
HEAD_DIM = 64
DA_HEADS = 4
DA_W = DA_HEADS * 2 * HEAD_DIM
NA_HEADS = 8
NA_PAIRS = NA_HEADS // 2
NA_W = NA_HEADS * HEAD_DIM
GRID_W = 64
NA_ROWS = 8
NA_COLS = 16
D_FF = 2816
IN_COLS = 5120
EPS = 1e-6
NEG = -1e30
LANES = 128
LOG2E = math.log2(math.e)
LAMBDA_INIT = 0.8 - 0.6 * math.exp(-0.3 * 0)

CB_KA, CB_QB, CB_KB, CB_VB, CB_GA, CB_GB = 0, 4, 8, 12, 16, 24
N_COLBLK_N = 32
RB_QA, RB_VA = 0, 4

VMEM_LIMIT = 56 * 1024 * 1024


def _rms(xf, w):
    return xf * lax.rsqrt(jnp.mean(xf * xf, axis=-1, keepdims=True) + EPS) * w


IN_TM = 512
IN_TN = 1024
IN_NBLK = IN_TN // LANES
Q_SCALE = HEAD_DIM ** -0.5 * LOG2E
STAT_ROWS = 8


def _max_group_sqnorm(a, grp):
    ab = a.astype(jnp.bfloat16).astype(jnp.float32)
    n2 = jnp.dot((ab * ab).astype(jnp.bfloat16), grp, preferred_element_type=jnp.float32)
    return jnp.max(n2, axis=0, keepdims=True)


def _in_proj_kernel(x_ref, nw_ref, w_ref, b_ref, grp_ref, ot_ref, on_ref, st_ref):
    h = _rms(x_ref[...], nw_ref[...]).astype(jnp.bfloat16)

    def chunk(c):
        return jnp.dot(h, w_ref[:, c * IN_TN:(c + 1) * IN_TN], preferred_element_type=jnp.float32)

    def store_natural(c, val):
        for j in range(IN_NBLK):
            on_ref[(c - 1) * IN_NBLK + j] = val[:, j * LANES:(j + 1) * LANES].astype(on_ref.dtype)

    acc = chunk(0)
    qa = acc[:, :DA_W] * Q_SCALE
    for j in range(IN_NBLK // 2):
        ot_ref[RB_QA + j, 0] = qa[:, j * LANES:(j + 1) * LANES].T.astype(ot_ref.dtype)
        ot_ref[RB_VA + j, 0] = acc[:, DA_W + j * LANES:DA_W + (j + 1) * LANES].T.astype(ot_ref.dtype)
    q_stat = _max_group_sqnorm(qa, grp_ref[...])
    acc = chunk(1)
    k_stat = _max_group_sqnorm(acc[:, :DA_W], grp_ref[...])
    qb = acc[:, DA_W:] * Q_SCALE
    store_natural(1, jnp.concatenate([acc[:, :DA_W], qb], axis=1))
    qb_stat = _max_group_sqnorm(qb, grp_ref[...])
    acc = chunk(2)
    store_natural(2, acc)
    kb_stat = _max_group_sqnorm(acc[:, :NA_W], grp_ref[...])
    for c in (3, 4):
        gb = b_ref[:, (c - 3) * IN_TN:(c - 2) * IN_TN]
        store_natural(c, jax.nn.sigmoid(chunk(c) + gb))
    st_ref[0] = jnp.concatenate(
        [k_stat, q_stat, kb_stat, qb_stat, jnp.zeros((STAT_ROWS - 4, LANES), jnp.float32)], axis=0)


def _in_proj(x2, nw, w_bf, b_gate_row, grp):
    m = x2.shape[0]
    nt = m // IN_TM
    const = lambda i: (0, 0)
    return pl.pallas_call(
        _in_proj_kernel,
        grid=(nt,),
        in_specs=[
            pl.BlockSpec((IN_TM, D_MODEL), lambda i: (i, 0)),
            pl.BlockSpec((1, D_MODEL), const),
            pl.BlockSpec((D_MODEL, IN_COLS), const, pipeline_mode=pl.Buffered(1)),
            pl.BlockSpec((1, 2 * D_MODEL), const),
            pl.BlockSpec((DA_W, LANES), const),
        ],
        out_specs=[
            pl.BlockSpec((IN_NBLK, 1, LANES, IN_TM), lambda i: (0, i, 0, 0)),
            pl.BlockSpec((N_COLBLK_N, IN_TM, LANES), lambda i: (0, i, 0)),
            pl.BlockSpec((1, STAT_ROWS, LANES), lambda i: (i, 0, 0)),
        ],
        out_shape=[
            jax.ShapeDtypeStruct((IN_NBLK, nt, LANES, IN_TM), jnp.bfloat16),
            jax.ShapeDtypeStruct((N_COLBLK_N, m, LANES), jnp.bfloat16),
            jax.ShapeDtypeStruct((nt, STAT_ROWS, LANES), jnp.float32),
        ],
        compiler_params=pltpu.CompilerParams(
            dimension_semantics=("arbitrary",),
            vmem_limit_bytes=VMEM_LIMIT),
        name="in_proj",
    )(x2, nw, w_bf, b_gate_row, grp)


DA_TQ = 1024
DA_TK = 512
DA_QG = 256
DA_NG = DA_TQ // DA_QG
DA_QT = DA_TQ // IN_TM
DA_DIAG = DA_TQ // DA_TK
DA_NAUG = 9
BF16_EXACT_INT = 256
DA_AHEAD = 4
DA_SKIP_LOG2 = 127.0
DA_FIXED_REACH = 64.0
DA_NBOUND = 5
DA_NEAR_LEFT = ((1, 0), (1, 1))
DA_NEAR_RIGHT = ((0, 3), (0, 2))
DA_NORM_MARGIN = 1.01


def _alibi_tables(slopes):
    f32, bf = np.float32, jnp.bfloat16
    s2 = slopes.astype(f32) * f32(LOG2E)
    c1 = s2.astype(bf)
    c2 = (s2 - c1.astype(f32)).astype(bf)
    c3 = (s2 - c1.astype(f32) - c2.astype(f32)).astype(bf)
    parts = np.stack([c1, c2, c3], axis=1).astype(f32)
    s2_used = parts[:, 0] + parts[:, 1] + parts[:, 2]
    dk = np.arange(DA_TK)
    dk_lo = (dk % BF16_EXACT_INT).astype(f32)
    dk_hi = (dk - dk % BF16_EXACT_INT).astype(f32)
    ones3 = np.ones((3,), f32)
    k_aug = np.concatenate([
        np.broadcast_to((dk_lo[:, None] * ones3)[None], (DA_HEADS, DA_TK, 3)),
        np.broadcast_to((dk_hi[:, None] * ones3)[None], (DA_HEADS, DA_TK, 3)),
        np.broadcast_to(parts[:, None, :], (DA_HEADS, DA_TK, 3)),
    ], axis=-1)
    pad_k = np.zeros((DA_HEADS, DA_TK, HEAD_DIM - DA_NAUG), f32)
    k_aug = np.concatenate([k_aug, pad_k], axis=-1)
    zeros_k = np.zeros((DA_HEADS, DA_TK, HEAD_DIM), f32)
    k_tab = np.stack([np.concatenate([zeros_k, k_aug], axis=-1),
                      np.concatenate([k_aug, zeros_k], axis=-1)], axis=1)
    dq = (np.arange(DA_TQ) % DA_QG).astype(f32)
    sig = np.asarray([1.0, -1.0], f32)
    q_par = sig[None, :, None, None] * parts[:, None, :, None] * np.ones((DA_TQ,), f32)
    q_off = -sig[None, :, None, None] * np.broadcast_to(dq, (DA_HEADS, 1, 3, DA_TQ))
    q_aug = np.concatenate([q_par, q_par, q_off], axis=2)
    q_aug = np.concatenate(
        [q_aug, np.zeros((DA_HEADS, 2, HEAD_DIM - DA_NAUG, DA_TQ), f32)], axis=2)
    return k_tab.astype(bf), q_aug.astype(bf), s2_used


def _diff_attn_kernel(s2_ref, bounds_ref, refpt_ref, q_ref, k_ref, v_ref, ktab_ref, qaug_ref, lam_ref, sw_ref, o_ref,
                      ka_ref, qm_ref, dist_ref, m_ref, l_ref, acc_ref, *, nkc):
    h = pl.program_id(1)
    qi = pl.program_id(2)
    slope2 = s2_ref[h]

    @pl.when(qi == 0)
    def _():
        lane = lax.broadcasted_iota(jnp.int32, (DA_TK, LANES), 1)
        for t in range(nkc):
            rows = slice(t * DA_TK, (t + 1) * DA_TK)
            kt = k_ref[rows, :]
            ka_ref[0, rows, :] = jnp.where(lane < HEAD_DIM, kt, ktab_ref[0])
            ka_ref[1, rows, :] = jnp.where(lane >= HEAD_DIM, kt, ktab_ref[1])

    for t in range(DA_QT):
        qt = q_ref[t]
        cols = slice(t * IN_TM, (t + 1) * IN_TM)
        for sg in range(2):
            qm_ref[sg, 0, :HEAD_DIM, cols] = qt[:HEAD_DIM]
            qm_ref[sg, 0, HEAD_DIM:, cols] = qaug_ref[sg, :, cols]
            qm_ref[sg, 1, :HEAD_DIM, cols] = qaug_ref[sg, :, cols]
            qm_ref[sg, 1, HEAD_DIM:, cols] = qt[HEAD_DIM:]
    dq = lax.broadcasted_iota(jnp.int32, (DA_TK, DA_QG), 1)
    dk = lax.broadcasted_iota(jnp.int32, (DA_TK, DA_QG), 0)
    dist_ref[...] = (dq - dk).astype(jnp.float32) * slope2
    m_ref[...] = jnp.full_like(m_ref, -jnp.inf)
    l_ref[...] = jnp.zeros_like(l_ref)
    acc_ref[...] = jnp.zeros_like(acc_ref)

    step = (pl.program_id(0) * DA_HEADS + h) * pl.num_programs(2) + qi

    def make_body(sg, diagonal, fixed_ref=False, positions=None):
        tiles = [(sub, g, mp) for sub in range(DA_DIAG) for g in range(DA_NG) for mp in range(2)
                 if positions is None or (sub, g) in positions]

        def side(sub, g):
            if not diagonal:
                return sg, (1.0 if sg == 0 else -1.0), False
            c = g * DA_QG - sub * DA_TK
            if c >= DA_TK:
                return 0, 1.0, False
            if c <= -DA_QG:
                return 1, -1.0, False
            return 0, 1.0, True

        def body(kb, carry):
            def origin(sub, g):
                kc = kb * DA_DIAG + sub
                return (qi * DA_TQ + g * DA_QG - kc * DA_TK).astype(jnp.float32) * slope2

            corrections = {}

            def correction(sub, g):
                if (sub, g) not in corrections:
                    corrections[sub, g] = 2.0 * jnp.minimum(dist_ref[...] + origin(sub, g), 0.0)
                return corrections[sub, g]

            def scores(sub, g, mp):
                cols = slice(g * DA_QG, (g + 1) * DA_QG)
                k0 = pl.multiple_of((kb * DA_DIAG + sub) * DA_TK, DA_TK)
                variant, _, straddles = side(sub, g)
                st = jnp.dot(ka_ref[mp, pl.ds(k0, DA_TK), :], qm_ref[variant, mp, :, cols],
                             preferred_element_type=jnp.float32)
                return st + correction(sub, g) if straddles else st

            ahead = [scores(*tiles[i]) for i in range(min(DA_AHEAD, len(tiles)))]
            for ti, (sub, g, mp) in enumerate(tiles):
                cols = slice(g * DA_QG, (g + 1) * DA_QG)
                st = ahead.pop(0)
                if ti + DA_AHEAD < len(tiles):
                    ahead.append(scores(*tiles[ti + DA_AHEAD]))
                shift = side(sub, g)[1] * origin(sub, g)
                vt = v_ref[kb * DA_DIAG + sub]
                if fixed_ref:
                    p = jnp.exp2(st - (refpt_ref[step * 2 + mp] + shift))
                    l_ref[mp, :, cols] = l_ref[mp, :, cols] + jnp.sum(p, axis=0, keepdims=True)
                    acc_ref[mp, :, cols] = acc_ref[mp, :, cols] + jnp.dot(
                        vt, p.astype(jnp.bfloat16), preferred_element_type=jnp.float32)
                else:
                    m_old = m_ref[mp, :, cols]
                    m_new = jnp.maximum(m_old, jnp.max(st, axis=0, keepdims=True) - shift)
                    alpha = jnp.exp2(m_old - m_new)
                    p = jnp.exp2(st - (m_new + shift))
                    l_ref[mp, :, cols] = alpha * l_ref[mp, :, cols] + jnp.sum(p, axis=0, keepdims=True)
                    acc_ref[mp, :, cols] = alpha * acc_ref[mp, :, cols] + jnp.dot(
                        vt, p.astype(jnp.bfloat16), preferred_element_type=jnp.float32)
                    m_ref[mp, :, cols] = m_new
            return carry

        return body

    lo = bounds_ref[step * DA_NBOUND]
    hi = bounds_ref[step * DA_NBOUND + 1]
    fixed_ok = bounds_ref[step * DA_NBOUND + 2]
    near_l = bounds_ref[step * DA_NBOUND + 3]
    near_r = bounds_ref[step * DA_NBOUND + 4]

    @pl.when(fixed_ok == 1)
    def _():
        make_body(0, True, True)(qi, 0)
        lax.fori_loop(lo, qi, make_body(0, False, True), 0)
        lax.fori_loop(qi + 1, hi, make_body(1, False, True), 0)

        @pl.when(near_l == 1)
        def _():
            make_body(0, False, True, DA_NEAR_LEFT)(lo - 1, 0)

        @pl.when(near_r == 1)
        def _():
            make_body(1, False, True, DA_NEAR_RIGHT)(hi, 0)

    @pl.when(fixed_ok != 1)
    def _():
        make_body(0, True)(qi, 0)
        lax.fori_loop(lo - near_l, qi, make_body(0, False), 0)
        lax.fori_loop(qi + 1, hi + near_r, make_body(1, False), 0)

    lv = lam_ref[...]
    lam = (jnp.exp(jnp.sum(lv[0:1] * lv[1:2], axis=-1, keepdims=True))
           - jnp.exp(jnp.sum(lv[2:3] * lv[3:4], axis=-1, keepdims=True))
           + LAMBDA_INIT)
    o = acc_ref[0] / l_ref[0] - lam * (acc_ref[1] / l_ref[1])
    o = o * lax.rsqrt(jnp.mean(o * o, axis=0, keepdims=True) + EPS) * sw_ref[...] * (1.0 - LAMBDA_INIT)
    o_ref[...] = o.T.astype(o_ref.dtype)


def _skip_bounds(stats, s2_used, batch, seq):
    nq, nkc = seq // DA_TQ, seq // DA_TK
    nkb = nkc // DA_DIAG
    norms = DA_NORM_MARGIN * jnp.sqrt(stats[:, :2, :2 * DA_HEADS])
    norms = norms.reshape(batch, nkc, 2, DA_HEADS, 2)
    kn = norms[:, :, 0]
    qn = norms[:, :, 1].reshape(batch, nq, DA_QT, DA_HEADS, 2).max(axis=2)
    kn_self = kn.reshape(batch, nq, DA_DIAG, DA_HEADS, 2).max(axis=2)
    reach = qn[:, :, None] * (kn[:, None, :] + kn_self[:, :, None])
    q_lo = (np.arange(nq)[:, None] * DA_TQ + np.arange(DA_NG)[None, :] * DA_QG)[:, :, None]
    k_lo = (np.arange(nkc) * DA_TK)[None, None, :]
    gap = np.maximum(np.maximum(q_lo - (k_lo + DA_TK - 1), k_lo - (q_lo + DA_QG - 1)), 0)
    bound = (reach[:, :, None] - s2_used[None, None, None, None, :, None]
             * gap[None, :, :, :, None, None].astype(np.float32))
    need = jnp.any(~(bound < -DA_SKIP_LOG2), axis=-1) | (gap == 0)[None, :, :, :, None]
    need = need.reshape(batch, nq, DA_NG, nkb, DA_DIAG, DA_HEADS)
    near = np.zeros((2, DA_NG, DA_DIAG), bool)
    for side, positions in enumerate((DA_NEAR_LEFT, DA_NEAR_RIGHT)):
        for sub, g in positions:
            near[side, g, sub] = True
    blk = np.arange(nkb)
    left = (blk[None, :] < np.arange(nq)[:, None])[:, None, :, None]
    right = (blk[None, :] > np.arange(nq)[:, None])[:, None, :, None]
    in_near = (left & near[0][None, :, None, :]) | (right & near[1][None, :, None, :])
    need_any = need.any(axis=(2, 4))
    need_far = (need & ~in_near[None, :, :, :, :, None]).any(axis=(2, 4))
    blk = blk[None, None, :, None]
    lo_far = jnp.min(jnp.where(need_far, blk, nkb), axis=2)
    lo_any = jnp.min(jnp.where(need_any, blk, nkb), axis=2)
    hi_far = jnp.max(jnp.where(need_far, blk + 1, 0), axis=2)
    hi_any = jnp.max(jnp.where(need_any, blk + 1, 0), axis=2)
    near_l = lo_any == lo_far - 1
    near_r = hi_any == hi_far + 1
    lo = jnp.where(lo_any >= lo_far - 1, lo_far, lo_any)
    hi = jnp.where(hi_any <= hi_far + 1, hi_far, hi_any)
    refpt = -(qn * kn_self)
    reach_blk = reach.reshape(batch, nq, nkb, DA_DIAG, DA_HEADS, 2).max(axis=(3, 5))
    visited = (blk >= jnp.minimum(lo, lo_any)[:, :, None]) & (blk < jnp.maximum(hi, hi_any)[:, :, None])
    fixed_ok = jnp.all(~visited | (reach_blk <= DA_FIXED_REACH), axis=2)
    bounds = jnp.stack([lo, hi, fixed_ok.astype(lo.dtype), near_l.astype(lo.dtype),
                        near_r.astype(lo.dtype)], axis=-1)
    return (jnp.transpose(bounds, (0, 2, 1, 3)).reshape(-1).astype(jnp.int32),
            jnp.transpose(refpt, (0, 2, 1, 3)).reshape(-1).astype(jnp.float32))


def _diff_attn(proj_t, proj_n, stats, slopes, lam_vecs, subln_col, batch, seq):
    nq = seq // DA_TQ
    nkc = seq // DA_TK
    m = batch * seq
    k_tab, q_aug, s2_used = _alibi_tables(slopes)
    bounds, refpt = _skip_bounds(stats, s2_used, batch, seq)
    grid_spec = pltpu.PrefetchScalarGridSpec(
        num_scalar_prefetch=3,
        grid=(batch, DA_HEADS, nq),
        in_specs=[
            pl.BlockSpec((None, DA_QT, LANES, IN_TM), lambda b, h, qi, s, bd, rp: (RB_QA + h, b * nq + qi, 0, 0)),
            pl.BlockSpec((None, seq, LANES), lambda b, h, qi, s, bd, rp: (CB_KA + h, b, 0)),
            pl.BlockSpec((None, nkc, LANES, DA_TK), lambda b, h, qi, s, bd, rp: (RB_VA + h, b, 0, 0)),
            pl.BlockSpec((None, 2, DA_TK, LANES), lambda b, h, qi, s, bd, rp: (h, 0, 0, 0)),
            pl.BlockSpec((None, 2, HEAD_DIM, DA_TQ), lambda b, h, qi, s, bd, rp: (h, 0, 0, 0)),
            pl.BlockSpec((4, HEAD_DIM), lambda b, h, qi, s, bd, rp: (0, 0)),
            pl.BlockSpec((2 * HEAD_DIM, 1), lambda b, h, qi, s, bd, rp: (0, 0)),
        ],
        out_specs=pl.BlockSpec((None, DA_TQ, LANES), lambda b, h, qi, s, bd, rp: (h, b * nq + qi, 0)),
        scratch_shapes=[
            pltpu.VMEM((2, seq, LANES), jnp.bfloat16),
            pltpu.VMEM((2, 2, LANES, DA_TQ), jnp.bfloat16),
            pltpu.VMEM((DA_TK, DA_QG), jnp.float32),
            pltpu.VMEM((2, 1, DA_TQ), jnp.float32),
            pltpu.VMEM((2, 1, DA_TQ), jnp.float32),
            pltpu.VMEM((2, LANES, DA_TQ), jnp.float32),
        ],
    )
    return pl.pallas_call(
        functools.partial(_diff_attn_kernel, nkc=nkc),
        grid_spec=grid_spec,
        out_shape=jax.ShapeDtypeStruct((DA_HEADS, m, LANES), jnp.bfloat16),
        compiler_params=pltpu.CompilerParams(
            dimension_semantics=("arbitrary", "arbitrary", "arbitrary"),
            vmem_limit_bytes=VMEM_LIMIT),
        name="diff_attn",
    )(s2_used, bounds, refpt, proj_t, proj_n, proj_t, k_tab, q_aug, lam_vecs, subln_col)


NA_GROUP = 32
NA_TOK = NA_GROUP * GRID_W
NA_WIN = NA_ROWS * GRID_W
NA_FIXED_REACH = 64.0
NA_AHEAD = 2
NA_BIAS_ROWS = (2 * NA_ROWS - 1) * GRID_W


def _nbr_attn_kernel(ok_ref, refpt_ref, q_ref, k_ref, v_ref, bias_ref, o_ref, *, rows):
    g = pl.program_id(2)
    step = (pl.program_id(0) * NA_PAIRS + pl.program_id(1)) * pl.num_programs(2) + g
    lane_q = lax.broadcasted_iota(jnp.int32, (GRID_W, LANES), 1)
    dn_last = (((1,), (1,)), ((), ()))
    dn_first = (((0,), (0,)), ((), ()))
    ones = jnp.ones((NA_WIN, LANES), jnp.bfloat16)

    def key_start(rr):
        r = g * NA_GROUP + rr
        rs = jnp.clip(r - NA_ROWS // 2, 0, rows - NA_ROWS)
        return r, rs, pl.multiple_of(rs * GRID_W, GRID_W)

    def scores(rr):
        r, rs, k0 = key_start(rr)
        b0 = pl.multiple_of((NA_ROWS - 1 - (r - rs)) * GRID_W, GRID_W)
        q = q_ref[rr * GRID_W:(rr + 1) * GRID_W, :]
        zero = jnp.zeros_like(q)
        qbd = jnp.concatenate([jnp.where(lane_q < HEAD_DIM, q, zero),
                               jnp.where(lane_q >= HEAD_DIM, q, zero)], axis=0)
        return lax.dot_general(k_ref[pl.ds(k0, NA_WIN), :], qbd, dn_last,
                               preferred_element_type=jnp.float32) + bias_ref[pl.ds(b0, NA_WIN), :]

    def group(fixed_ref):
        if fixed_ref:
            lane_k = lax.broadcasted_iota(jnp.int32, (1, LANES), 1)
            ref_row = jnp.where(lane_k < HEAD_DIM, refpt_ref[step * 2], refpt_ref[step * 2 + 1])
        ahead = [scores(rr) for rr in range(NA_AHEAD)]
        for rr in range(NA_GROUP):
            st = ahead.pop(0)
            if rr + NA_AHEAD < NA_GROUP:
                ahead.append(scores(rr + NA_AHEAD))
            p = jnp.exp2(st - (ref_row if fixed_ref else jnp.max(st, axis=0, keepdims=True)))
            v_ext = jnp.concatenate([v_ref[pl.ds(key_start(rr)[2], NA_WIN), :], ones], axis=1)
            full = lax.dot_general(p.astype(jnp.bfloat16), v_ext, dn_first,
                                   preferred_element_type=jnp.float32)
            num = jnp.where(lane_q < HEAD_DIM, full[:GRID_W, :LANES], full[GRID_W:, :LANES])
            den = jnp.where(lane_q < HEAD_DIM, full[:GRID_W, LANES:], full[GRID_W:, LANES:])
            o_ref[rr * GRID_W:(rr + 1) * GRID_W, :] = (num / den).astype(o_ref.dtype)

    for fixed in (True, False):
        pl.when((ok_ref[step] == 1) if fixed else (ok_ref[step] != 1))(functools.partial(group, fixed))


def _nbr_ref_points(stats, rpb_l, batch, seq):
    ng = seq // NA_TOK
    nt = seq // IN_TM
    assert NA_ROWS * GRID_W <= IN_TM, "a query's window must stay inside its own and the adjacent tiles"
    norms = DA_NORM_MARGIN * jnp.sqrt(stats[:, 2:4, :NA_HEADS]).reshape(batch, nt, 2, NA_HEADS)
    kn, qn = norms[:, :, 0], norms[:, :, 1]
    pad = jnp.pad(kn, ((0, 0), (1, 1), (0, 0)))
    kn_win = jnp.maximum(jnp.maximum(pad[:, :-2], pad[:, 1:-1]), pad[:, 2:])
    kn_win = kn_win.reshape(batch, ng, nt // ng, NA_HEADS).max(axis=2)
    qn = qn.reshape(batch, ng, nt // ng, NA_HEADS).max(axis=2)
    bias = rpb_l.astype(jnp.float32) * LOG2E
    b_self = bias[:, NA_ROWS - 1, NA_COLS - 1]
    b_max = jnp.max(bias, axis=(1, 2))
    refpt = -(qn * kn_win) + b_self
    spread = 2.0 * qn * kn_win + (b_max - b_self)
    ok = jnp.all((spread <= NA_FIXED_REACH).reshape(batch, ng, NA_PAIRS, 2), axis=-1)
    refpt = jnp.transpose(refpt.reshape(batch, ng, NA_PAIRS, 2), (0, 2, 1, 3))
    return (jnp.transpose(ok, (0, 2, 1)).reshape(-1).astype(jnp.int32),
            refpt.reshape(-1).astype(jnp.float32))


def _nbr_attn(proj, stats, rpb_l, batch, seq):
    rows = seq // GRID_W
    ng = seq // NA_TOK
    m = batch * seq
    ok, refpt = _nbr_ref_points(stats, rpb_l, batch, seq)
    grid_spec = pltpu.PrefetchScalarGridSpec(
        num_scalar_prefetch=2,
        grid=(batch, NA_PAIRS, ng),
        in_specs=[
            pl.BlockSpec((None, NA_TOK, LANES), lambda b, hp, g, ok, rp: (CB_QB + hp, b * ng + g, 0)),
            pl.BlockSpec((None, seq, LANES), lambda b, hp, g, ok, rp: (CB_KB + hp, b, 0)),
            pl.BlockSpec((None, seq, LANES), lambda b, hp, g, ok, rp: (CB_VB + hp, b, 0)),
            pl.BlockSpec((None, NA_BIAS_ROWS, LANES), lambda b, hp, g, ok, rp: (hp, 0, 0)),
        ],
        out_specs=pl.BlockSpec((None, NA_TOK, LANES), lambda b, hp, g, ok, rp: (hp, b * ng + g, 0)),
    )
    return pl.pallas_call(
        functools.partial(_nbr_attn_kernel, rows=rows),
        grid_spec=grid_spec,
        out_shape=jax.ShapeDtypeStruct((NA_PAIRS, m, LANES), jnp.bfloat16),
        compiler_params=pltpu.CompilerParams(
            dimension_semantics=("arbitrary", "arbitrary", "arbitrary"),
            vmem_limit_bytes=VMEM_LIMIT),
        name="nbr_attn",
    )(ok, refpt, proj, proj, proj, _nbr_bias_table(rpb_l))


def _nbr_bias_table(rpb_l):
    col = np.arange(GRID_W)
    col_start = np.clip(col - NA_COLS // 2, 0, GRID_W - NA_COLS)
    col_in = (col[None, :] >= col_start[:, None]) & (col[None, :] < col_start[:, None] + NA_COLS)
    col_off = np.clip(col[None, :] - col[:, None] + NA_COLS - 1, 0, 2 * NA_COLS - 2)
    pick = (col_off[None] == np.arange(2 * NA_COLS - 1)[:, None, None]).astype(np.float32)
    t = jnp.einsum('hic,cqk->hiqk', rpb_l.astype(jnp.float32), pick,
                   precision=lax.Precision.HIGHEST)
    t = jnp.where(col_in[None, None], t * LOG2E, NEG)
    t = t.reshape(NA_PAIRS, 2, 2 * NA_ROWS - 1, GRID_W, GRID_W)
    t = jnp.transpose(t, (0, 2, 4, 1, 3))
    return t.reshape(NA_PAIRS, NA_BIAS_ROWS, LANES)


MIX_TM = 1024


def _cat_blocks(ref):
    return jnp.concatenate([ref[c] for c in range(ref.shape[0])], axis=-1)


def _mix_out_kernel(x_ref, ya_ref, yb_ref, ga_ref, gb_ref, wa_ref, wb_ref, wo_ref, nw_ref, o_ref):
    ya = _cat_blocks(ya_ref)
    yb = _cat_blocks(yb_ref)
    ga = _cat_blocks(ga_ref).astype(jnp.float32)
    gb = _cat_blocks(gb_ref).astype(jnp.float32)
    merged = (ga * jnp.dot(ya, wa_ref[...], preferred_element_type=jnp.float32)
              + gb * jnp.dot(yb, wb_ref[...], preferred_element_type=jnp.float32))
    t = jnp.dot(merged.astype(jnp.bfloat16), wo_ref[...], preferred_element_type=jnp.float32)
    o_ref[...] = x_ref[...] + _rms(t, nw_ref[...])


def _mix_out(x2, ya, yb, proj, wa, wb, wo, nw):
    m = x2.shape[0]
    const = lambda i: (0, 0)
    gate_blk = D_MODEL // LANES
    return pl.pallas_call(
        _mix_out_kernel,
        grid=(m // MIX_TM,),
        in_specs=[
            pl.BlockSpec((MIX_TM, D_MODEL), lambda i: (i, 0)),
            pl.BlockSpec((DA_HEADS, MIX_TM, LANES), lambda i: (0, i, 0)),
            pl.BlockSpec((NA_PAIRS, MIX_TM, LANES), lambda i: (0, i, 0)),
            pl.BlockSpec((gate_blk, MIX_TM, LANES), lambda i: (CB_GA // gate_blk, i, 0)),
            pl.BlockSpec((gate_blk, MIX_TM, LANES), lambda i: (CB_GB // gate_blk, i, 0)),
            pl.BlockSpec(wa.shape, const),
            pl.BlockSpec(wb.shape, const),
            pl.BlockSpec(wo.shape, const),
            pl.BlockSpec((1, D_MODEL), const),
        ],
        out_specs=pl.BlockSpec((MIX_TM, D_MODEL), lambda i: (i, 0)),
        out_shape=jax.ShapeDtypeStruct((m, D_MODEL), jnp.float32),
        compiler_params=pltpu.CompilerParams(
            dimension_semantics=("arbitrary",),
            vmem_limit_bytes=VMEM_LIMIT),
        name="mix_out",
    )(x2, ya, yb, proj, proj, wa, wb, wo, nw)


FFN_TM = 512
FFN_CHUNK = 256
FFN_NCHUNK = D_FF // FFN_CHUNK


def _ffn_kernel(x_ref, nw1_ref, wg_ref, wu_ref, wd_ref, nw2_ref, o_ref):
    x = x_ref[...]
    h = _rms(x, nw1_ref[...]).astype(jnp.bfloat16)
    f = jnp.zeros((FFN_TM, D_MODEL), jnp.float32)
    for c in range(FFN_NCHUNK):
        sl = slice(c * FFN_CHUNK, (c + 1) * FFN_CHUNK)
        g = jnp.dot(h, wg_ref[:, sl], preferred_element_type=jnp.float32)
        u = jnp.dot(h, wu_ref[:, sl], preferred_element_type=jnp.float32)
        a = (jax.nn.silu(g) * u).astype(jnp.bfloat16)
        f = f + jnp.dot(a, wd_ref[sl, :], preferred_element_type=jnp.float32)
    o_ref[...] = x + _rms(f, nw2_ref[...])


def _ffn(x1, nw1, wg, wu, wd, nw2):
    m = x1.shape[0]
    const = lambda i: (0, 0)
    once = pl.Buffered(1)
    return pl.pallas_call(
        _ffn_kernel,
        grid=(m // FFN_TM,),
        in_specs=[
            pl.BlockSpec((FFN_TM, D_MODEL), lambda i: (i, 0)),
            pl.BlockSpec((1, D_MODEL), const),
            pl.BlockSpec(wg.shape, const, pipeline_mode=once),
            pl.BlockSpec(wu.shape, const, pipeline_mode=once),
            pl.BlockSpec(wd.shape, const, pipeline_mode=once),
            pl.BlockSpec((1, D_MODEL), const),
        ],
        out_specs=pl.BlockSpec((FFN_TM, D_MODEL), lambda i: (i, 0)),
        out_shape=jax.ShapeDtypeStruct((m, D_MODEL), jnp.float32),
        compiler_params=pltpu.CompilerParams(
            dimension_semantics=("arbitrary",),
            vmem_limit_bytes=VMEM_LIMIT),
        name="ffn",
    )(x1, nw1, wg, wu, wd, nw2)


def kernel(x, pre_mix_w, w_in, b_gate, lambda_q1, lambda_k1, lambda_q2, lambda_k2, subln_w, rpb,
           w_branch_a, w_branch_b, w_out, post_mix_w, pre_ffn_w, w_gate, w_up, w_down, post_ffn_w):
    batch, seq, _ = x.shape
    depth = w_in.shape[0]
    assert depth == 1, "LAMBDA_INIT is specialised to a single layer"
    bf = jnp.bfloat16
    f32 = jnp.float32
    slopes = np.asarray([2.0 ** (-8.0 * (i + 1) / DA_HEADS) for i in range(DA_HEADS)], np.float32)
    grp = (jnp.arange(DA_W)[:, None] // HEAD_DIM == jnp.arange(LANES)[None, :]).astype(bf)
    x2 = x.reshape(batch * seq, D_MODEL)
    for l in range(depth):
        w_l = w_in[l].astype(bf)
        w_p = jnp.concatenate([w_l[:, :DA_W], w_l[:, 2 * DA_W:3 * DA_W],
                               w_l[:, DA_W:2 * DA_W], w_l[:, 3 * DA_W:]], axis=1)
        proj_t, proj_n, stats = _in_proj(x2, pre_mix_w[l].reshape(1, D_MODEL).astype(f32), w_p,
                                         b_gate[l].reshape(1, 2 * D_MODEL).astype(f32), grp)
        lam_vecs = jnp.stack([lambda_q1[l], lambda_k1[l], lambda_q2[l], lambda_k2[l]]).astype(f32)
        ya = _diff_attn(proj_t, proj_n, stats, slopes, lam_vecs,
                        subln_w[l].reshape(2 * HEAD_DIM, 1).astype(f32), batch, seq)
        yb = _nbr_attn(proj_n, stats, rpb[l], batch, seq)
        x1 = _mix_out(x2, ya, yb, proj_n, w_branch_a[l].astype(bf), w_branch_b[l].astype(bf),
                      w_out[l].astype(bf), post_mix_w[l].reshape(1, D_MODEL).astype(f32))
        x2 = _ffn(x1, pre_ffn_w[l].reshape(1, D_MODEL).astype(f32), w_gate[l].astype(bf),
                  w_up[l].astype(bf), w_down[l].astype(bf), post_ffn_w[l].reshape(1, D_MODEL).astype(f32))
    return x2.reshape(batch, seq, D_MODEL)
```

```python
import functools
import math

import jax
import jax.numpy as jnp
import numpy as np
from jax import lax
from jax.experimental import pallas as pl
from jax.experimental.pallas import tpu as pltpu

D_MODEL = 1024
HEAD_DIM = 64
DA_HEADS = 4
DA_W = DA_HEADS * 2 * HEAD_DIM
NA_HEADS = 8
NA_PAIRS = NA_HEADS // 2
NA_W = NA_HEADS * HEAD_DIM
GRID_W = 64
NA_ROWS = 8
NA_COLS = 16
D_FF = 2816
IN_COLS = 5120
EPS = 1e-6
NEG = -1e30
LANES = 128
LOG2E = math.log2(math.e)
LAMBDA_INIT = 0.8 - 0.6 * math.exp(-0.3 * 0)

W_QA, W_KA, W_VA = 0, DA_W, 2 * DA_W
W_QB, W_KB, W_VB = 3 * DA_W, 3 * DA_W + NA_W, 3 * DA_W + 2 * NA_W
W_GATES = 3 * DA_W + 3 * NA_W
CB_KA, CB_QB, CB_KB, CB_VB, CB_GA, CB_GB = 0, 4, 8, 12, 16, 24
N_COLBLK_N = 32
RB_QA, RB_VA = 0, 4

VMEM_LIMIT = 56 * 1024 * 1024


def _rms(xf, w):
    return xf * lax.rsqrt(jnp.mean(xf * xf, axis=-1, keepdims=True) + EPS) * w


IN_TM = 512
IN_TN = 1024
IN_NBLK = IN_TN // LANES
Q_SCALE = HEAD_DIM ** -0.5 * LOG2E
STAT_ROWS = 8


def _max_group_sqnorm(a, grp):
    ab = a.astype(jnp.bfloat16).astype(jnp.float32)
    n2 = jnp.dot((ab * ab).astype(jnp.bfloat16), grp, preferred_element_type=jnp.float32)
    return jnp.max(n2, axis=0, keepdims=True)


def _in_proj_kernel(x_ref, nw_ref, w_ref, b_ref, grp_ref, ot_ref, on_ref, st_ref, wb_ref):
    @pl.when(pl.program_id(0) == 0)
    def _():
        for c in range(IN_COLS // IN_TN):
            wb_ref[:, c * IN_TN:(c + 1) * IN_TN] = w_ref[:, c * IN_TN:(c + 1) * IN_TN].astype(wb_ref.dtype)

    h = _rms(x_ref[...], nw_ref[...]).astype(jnp.bfloat16)

    def proj(col0, ncols):
        return jnp.dot(h, wb_ref[:, col0:col0 + ncols], preferred_element_type=jnp.float32)

    def store_natural(blk0, val):
        for j in range(val.shape[1] // LANES):
            on_ref[blk0 + j] = val[:, j * LANES:(j + 1) * LANES].astype(on_ref.dtype)

    def store_transposed(blk0, val):
        for j in range(val.shape[1] // LANES):
            ot_ref[blk0 + j, 0] = val[:, j * LANES:(j + 1) * LANES].T.astype(ot_ref.dtype)

    grp = grp_ref[...]
    qa = proj(W_QA, DA_W) * Q_SCALE
    store_transposed(RB_QA, qa)
    q_stat = _max_group_sqnorm(qa, grp)
    store_transposed(RB_VA, proj(W_VA, DA_W))
    ka = proj(W_KA, DA_W)
    store_natural(CB_KA, ka)
    k_stat = _max_group_sqnorm(ka, grp)
    qb = proj(W_QB, NA_W) * Q_SCALE
    store_natural(CB_QB, qb)
    qb_stat = _max_group_sqnorm(qb, grp)
    kv = proj(W_KB, 2 * NA_W)
    store_natural(CB_KB, kv)
    kb_stat = _max_group_sqnorm(kv[:, :NA_W], grp)
    for half in range(2):
        cols = slice(half * D_MODEL, (half + 1) * D_MODEL)
        store_natural(CB_GA + half * (D_MODEL // LANES),
                      jax.nn.sigmoid(proj(W_GATES + half * D_MODEL, D_MODEL) + b_ref[:, cols]))
    st_ref[0] = jnp.concatenate(
        [k_stat, q_stat, kb_stat, qb_stat, jnp.zeros((STAT_ROWS - 4, LANES), jnp.float32)], axis=0)


def _in_proj(x2, nw, w_f32, b_gate_row, grp):
    m = x2.shape[0]
    nt = m // IN_TM
    const = lambda i: (0, 0)
    return pl.pallas_call(
        _in_proj_kernel,
        grid=(nt,),
        in_specs=[
            pl.BlockSpec((IN_TM, D_MODEL), lambda i: (i, 0)),
            pl.BlockSpec((1, D_MODEL), const),
            pl.BlockSpec((D_MODEL, IN_COLS), const, pipeline_mode=pl.Buffered(1)),
            pl.BlockSpec((1, 2 * D_MODEL), const),
            pl.BlockSpec((DA_W, LANES), const),
        ],
        out_specs=[
            pl.BlockSpec((IN_NBLK, 1, LANES, IN_TM), lambda i: (0, i, 0, 0)),
            pl.BlockSpec((N_COLBLK_N, IN_TM, LANES), lambda i: (0, i, 0)),
            pl.BlockSpec((1, STAT_ROWS, LANES), lambda i: (i, 0, 0)),
        ],
        out_shape=[
            jax.ShapeDtypeStruct((IN_NBLK, nt, LANES, IN_TM), jnp.bfloat16),
            jax.ShapeDtypeStruct((N_COLBLK_N, m, LANES), jnp.bfloat16),
            jax.ShapeDtypeStruct((nt, STAT_ROWS, LANES), jnp.float32),
        ],
        scratch_shapes=[pltpu.VMEM((D_MODEL, IN_COLS), jnp.bfloat16)],
        compiler_params=pltpu.CompilerParams(
            dimension_semantics=("arbitrary",),
            vmem_limit_bytes=VMEM_LIMIT),
        name="in_proj",
    )(x2, nw, w_f32, b_gate_row, grp)


DA_TQ = 1024
DA_TK = 512
DA_QG = 256
DA_NG = DA_TQ // DA_QG
DA_QT = DA_TQ // IN_TM
DA_DIAG = DA_TQ // DA_TK
DA_NAUG = 9
BF16_EXACT_INT = 256
DA_AHEAD = 4
DA_SKIP_LOG2 = 127.0
DA_FIXED_REACH = 64.0
DA_NBOUND = 5
DA_NEAR_LEFT = ((1, 0), (1, 1))
DA_NEAR_RIGHT = ((0, 3), (0, 2))
DA_NORM_MARGIN = 1.01


def _alibi_tables(slopes):
    f32, bf = np.float32, jnp.bfloat16
    s2 = slopes.astype(f32) * f32(LOG2E)
    c1 = s2.astype(bf)
    c2 = (s2 - c1.astype(f32)).astype(bf)
    c3 = (s2 - c1.astype(f32) - c2.astype(f32)).astype(bf)
    parts = np.stack([c1, c2, c3], axis=1).astype(f32)
    s2_used = parts[:, 0] + parts[:, 1] + parts[:, 2]
    dk = np.arange(DA_TK)
    dk_lo = (dk % BF16_EXACT_INT).astype(f32)
    dk_hi = (dk - dk % BF16_EXACT_INT).astype(f32)
    ones3 = np.ones((3,), f32)
    k_aug = np.concatenate([
        np.broadcast_to((dk_lo[:, None] * ones3)[None], (DA_HEADS, DA_TK, 3)),
        np.broadcast_to((dk_hi[:, None] * ones3)[None], (DA_HEADS, DA_TK, 3)),
        np.broadcast_to(parts[:, None, :], (DA_HEADS, DA_TK, 3)),
    ], axis=-1)
    pad_k = np.zeros((DA_HEADS, DA_TK, HEAD_DIM - DA_NAUG), f32)
    k_aug = np.concatenate([k_aug, pad_k], axis=-1)
    zeros_k = np.zeros((DA_HEADS, DA_TK, HEAD_DIM), f32)
    k_tab = np.stack([np.concatenate([zeros_k, k_aug], axis=-1),
                      np.concatenate([k_aug, zeros_k], axis=-1)], axis=1)
    dq = (np.arange(DA_TQ) % DA_QG).astype(f32)
    sig = np.asarray([1.0, -1.0], f32)
    q_par = sig[None, :, None, None] * parts[:, None, :, None] * np.ones((DA_TQ,), f32)
    q_off = -sig[None, :, None, None] * np.broadcast_to(dq, (DA_HEADS, 1, 3, DA_TQ))
    q_aug = np.concatenate([q_par, q_par, q_off], axis=2)
    q_aug = np.concatenate(
        [q_aug, np.zeros((DA_HEADS, 2, HEAD_DIM - DA_NAUG, DA_TQ), f32)], axis=2)
    return k_tab.astype(bf), q_aug.astype(bf), s2_used


def _diff_attn_kernel(s2_ref, bounds_ref, refpt_ref, q_ref, k_ref, v_ref, ktab_ref, qaug_ref, lam_ref, sw_ref, o_ref,
                      ka_ref, qm_ref, dist_ref, m_ref, l_ref, acc_ref, *, nkc):
    h = pl.program_id(1)
    qi = pl.program_id(2)
    slope2 = s2_ref[h]

    @pl.when(qi == 0)
    def _():
        lane = lax.broadcasted_iota(jnp.int32, (DA_TK, LANES), 1)
        for t in range(nkc):
            rows = slice(t * DA_TK, (t + 1) * DA_TK)
            kt = k_ref[rows, :]
            ka_ref[0, rows, :] = jnp.where(lane < HEAD_DIM, kt, ktab_ref[0])
            ka_ref[1, rows, :] = jnp.where(lane >= HEAD_DIM, kt, ktab_ref[1])

    for t in range(DA_QT):
        qt = q_ref[t]
        cols = slice(t * IN_TM, (t + 1) * IN_TM)
        for sg in range(2):
            qm_ref[sg, 0, :HEAD_DIM, cols] = qt[:HEAD_DIM]
            qm_ref[sg, 0, HEAD_DIM:, cols] = qaug_ref[sg, :, cols]
            qm_ref[sg, 1, :HEAD_DIM, cols] = qaug_ref[sg, :, cols]
            qm_ref[sg, 1, HEAD_DIM:, cols] = qt[HEAD_DIM:]
    dq = lax.broadcasted_iota(jnp.int32, (DA_TK, DA_QG), 1)
    dk = lax.broadcasted_iota(jnp.int32, (DA_TK, DA_QG), 0)
    dist_ref[...] = (dq - dk).astype(jnp.float32) * slope2
    m_ref[...] = jnp.full_like(m_ref, -jnp.inf)
    l_ref[...] = jnp.zeros_like(l_ref)
    acc_ref[...] = jnp.zeros_like(acc_ref)

    step = (pl.program_id(0) * DA_HEADS + h) * pl.num_programs(2) + qi

    def make_body(sg, diagonal, fixed_ref=False, positions=None):
        tiles = [(sub, g, mp) for sub in range(DA_DIAG) for g in range(DA_NG) for mp in range(2)
                 if positions is None or (sub, g) in positions]

        def side(sub, g):
            if not diagonal:
                return sg, (1.0 if sg == 0 else -1.0), False
            c = g * DA_QG - sub * DA_TK
            if c >= DA_TK:
                return 0, 1.0, False
            if c <= -DA_QG:
                return 1, -1.0, False
            return 0, 1.0, True

        def body(kb, carry):
            def origin(sub, g):
                kc = kb * DA_DIAG + sub
                return (qi * DA_TQ + g * DA_QG - kc * DA_TK).astype(jnp.float32) * slope2

            corrections = {}

            def correction(sub, g):
                if (sub, g) not in corrections:
                    corrections[sub, g] = 2.0 * jnp.minimum(dist_ref[...] + origin(sub, g), 0.0)
                return corrections[sub, g]

            def scores(sub, g, mp):
                cols = slice(g * DA_QG, (g + 1) * DA_QG)
                k0 = pl.multiple_of((kb * DA_DIAG + sub) * DA_TK, DA_TK)
                variant, _, straddles = side(sub, g)
                st = jnp.dot(ka_ref[mp, pl.ds(k0, DA_TK), :], qm_ref[variant, mp, :, cols],
                             preferred_element_type=jnp.float32)
                return st + correction(sub, g) if straddles else st

            ahead = [scores(*tiles[i]) for i in range(min(DA_AHEAD, len(tiles)))]
            for ti, (sub, g, mp) in enumerate(tiles):
                cols = slice(g * DA_QG, (g + 1) * DA_QG)
                st = ahead.pop(0)
                if ti + DA_AHEAD < len(tiles):
                    ahead.append(scores(*tiles[ti + DA_AHEAD]))
                shift = side(sub, g)[1] * origin(sub, g)
                vt = v_ref[kb * DA_DIAG + sub]
                if fixed_ref:
                    p = jnp.exp2(st - (refpt_ref[step * 2 + mp] + shift))
                    l_ref[mp, :, cols] = l_ref[mp, :, cols] + jnp.sum(p, axis=0, keepdims=True)
                    acc_ref[mp, :, cols] = acc_ref[mp, :, cols] + jnp.dot(
                        vt, p.astype(jnp.bfloat16), preferred_element_type=jnp.float32)
                else:
                    m_old = m_ref[mp, :, cols]
                    m_new = jnp.maximum(m_old, jnp.max(st, axis=0, keepdims=True) - shift)
                    alpha = jnp.exp2(m_old - m_new)
                    p = jnp.exp2(st - (m_new + shift))
                    l_ref[mp, :, cols] = alpha * l_ref[mp, :, cols] + jnp.sum(p, axis=0, keepdims=True)
                    acc_ref[mp, :, cols] = alpha * acc_ref[mp, :, cols] + jnp.dot(
                        vt, p.astype(jnp.bfloat16), preferred_element_type=jnp.float32)
                    m_ref[mp, :, cols] = m_new
            return carry

        return body

    lo = bounds_ref[step * DA_NBOUND]
    hi = bounds_ref[step * DA_NBOUND + 1]
    fixed_ok = bounds_ref[step * DA_NBOUND + 2]
    near_l = bounds_ref[step * DA_NBOUND + 3]
    near_r = bounds_ref[step * DA_NBOUND + 4]

    @pl.when(fixed_ok == 1)
    def _():
        make_body(0, True, True)(qi, 0)
        lax.fori_loop(lo, qi, make_body(0, False, True), 0)
        lax.fori_loop(qi + 1, hi, make_body(1, False, True), 0)

        @pl.when(near_l == 1)
        def _():
            make_body(0, False, True, DA_NEAR_LEFT)(lo - 1, 0)

        @pl.when(near_r == 1)
        def _():
            make_body(1, False, True, DA_NEAR_RIGHT)(hi, 0)

    @pl.when(fixed_ok != 1)
    def _():
        make_body(0, True)(qi, 0)
        lax.fori_loop(lo - near_l, qi, make_body(0, False), 0)
        lax.fori_loop(qi + 1, hi + near_r, make_body(1, False), 0)

    lv = lam_ref[...]
    lam = (jnp.exp(jnp.sum(lv[0:1] * lv[1:2], axis=-1, keepdims=True))
           - jnp.exp(jnp.sum(lv[2:3] * lv[3:4], axis=-1, keepdims=True))
           + LAMBDA_INIT)
    o = acc_ref[0] / l_ref[0] - lam * (acc_ref[1] / l_ref[1])
    o = o * lax.rsqrt(jnp.mean(o * o, axis=0, keepdims=True) + EPS) * sw_ref[...] * (1.0 - LAMBDA_INIT)
    o_ref[...] = o.T.astype(o_ref.dtype)


def _skip_bounds(stats, s2_used, batch, seq):
    nq, nkc = seq // DA_TQ, seq // DA_TK
    nkb = nkc // DA_DIAG
    norms = DA_NORM_MARGIN * jnp.sqrt(stats[:, :2, :2 * DA_HEADS])
    norms = norms.reshape(batch, nkc, 2, DA_HEADS, 2)
    kn = norms[:, :, 0]
    qn = norms[:, :, 1].reshape(batch, nq, DA_QT, DA_HEADS, 2).max(axis=2)
    kn_self = kn.reshape(batch, nq, DA_DIAG, DA_HEADS, 2).max(axis=2)
    reach = qn[:, :, None] * (kn[:, None, :] + kn_self[:, :, None])
    q_lo = (np.arange(nq)[:, None] * DA_TQ + np.arange(DA_NG)[None, :] * DA_QG)[:, :, None]
    k_lo = (np.arange(nkc) * DA_TK)[None, None, :]
    gap = np.maximum(np.maximum(q_lo - (k_lo + DA_TK - 1), k_lo - (q_lo + DA_QG - 1)), 0)
    bound = (reach[:, :, None] - s2_used[None, None, None, None, :, None]
             * gap[None, :, :, :, None, None].astype(np.float32))
    need = jnp.any(~(bound < -DA_SKIP_LOG2), axis=-1) | (gap == 0)[None, :, :, :, None]
    need = need.reshape(batch, nq, DA_NG, nkb, DA_DIAG, DA_HEADS)
    near = np.zeros((2, DA_NG, DA_DIAG), bool)
    for side, positions in enumerate((DA_NEAR_LEFT, DA_NEAR_RIGHT)):
        for sub, g in positions:
            near[side, g, sub] = True
    blk = np.arange(nkb)
    left = (blk[None, :] < np.arange(nq)[:, None])[:, None, :, None]
    right = (blk[None, :] > np.arange(nq)[:, None])[:, None, :, None]
    in_near = (left & near[0][None, :, None, :]) | (right & near[1][None, :, None, :])
    need_any = need.any(axis=(2, 4))
    need_far = (need & ~in_near[None, :, :, :, :, None]).any(axis=(2, 4))
    blk = blk[None, None, :, None]
    lo_far = jnp.min(jnp.where(need_far, blk, nkb), axis=2)
    lo_any = jnp.min(jnp.where(need_any, blk, nkb), axis=2)
    hi_far = jnp.max(jnp.where(need_far, blk + 1, 0), axis=2)
    hi_any = jnp.max(jnp.where(need_any, blk + 1, 0), axis=2)
    near_l = lo_any == lo_far - 1
    near_r = hi_any == hi_far + 1
    lo = jnp.where(lo_any >= lo_far - 1, lo_far, lo_any)
    hi = jnp.where(hi_any <= hi_far + 1, hi_far, hi_any)
    refpt = -(qn * kn_self)
    reach_blk = reach.reshape(batch, nq, nkb, DA_DIAG, DA_HEADS, 2).max(axis=(3, 5))
    visited = (blk >= jnp.minimum(lo, lo_any)[:, :, None]) & (blk < jnp.maximum(hi, hi_any)[:, :, None])
    fixed_ok = jnp.all(~visited | (reach_blk <= DA_FIXED_REACH), axis=2)
    bounds = jnp.stack([lo, hi, fixed_ok.astype(lo.dtype), near_l.astype(lo.dtype),
                        near_r.astype(lo.dtype)], axis=-1)
    return (jnp.transpose(bounds, (0, 2, 1, 3)).reshape(-1).astype(jnp.int32),
            jnp.transpose(refpt, (0, 2, 1, 3)).reshape(-1).astype(jnp.float32))


def _diff_attn(proj_t, proj_n, stats, slopes, lam_vecs, subln_col, batch, seq):
    nq = seq // DA_TQ
    nkc = seq // DA_TK
    m = batch * seq
    k_tab, q_aug, s2_used = _alibi_tables(slopes)
    bounds, refpt = _skip_bounds(stats, s2_used, batch, seq)
    grid_spec = pltpu.PrefetchScalarGridSpec(
        num_scalar_prefetch=3,
        grid=(batch, DA_HEADS, nq),
        in_specs=[
            pl.BlockSpec((None, DA_QT, LANES, IN_TM), lambda b, h, qi, s, bd, rp: (RB_QA + h, b * nq + qi, 0, 0)),
            pl.BlockSpec((None, seq, LANES), lambda b, h, qi, s, bd, rp: (CB_KA + h, b, 0)),
            pl.BlockSpec((None, nkc, LANES, DA_TK), lambda b, h, qi, s, bd, rp: (RB_VA + h, b, 0, 0)),
            pl.BlockSpec((None, 2, DA_TK, LANES), lambda b, h, qi, s, bd, rp: (h, 0, 0, 0)),
            pl.BlockSpec((None, 2, HEAD_DIM, DA_TQ), lambda b, h, qi, s, bd, rp: (h, 0, 0, 0)),
            pl.BlockSpec((4, HEAD_DIM), lambda b, h, qi, s, bd, rp: (0, 0)),
            pl.BlockSpec((2 * HEAD_DIM, 1), lambda b, h, qi, s, bd, rp: (0, 0)),
        ],
        out_specs=pl.BlockSpec((None, DA_TQ, LANES), lambda b, h, qi, s, bd, rp: (h, b * nq + qi, 0)),
        scratch_shapes=[
            pltpu.VMEM((2, seq, LANES), jnp.bfloat16),
            pltpu.VMEM((2, 2, LANES, DA_TQ), jnp.bfloat16),
            pltpu.VMEM((DA_TK, DA_QG), jnp.float32),
            pltpu.VMEM((2, 1, DA_TQ), jnp.float32),
            pltpu.VMEM((2, 1, DA_TQ), jnp.float32),
            pltpu.VMEM((2, LANES, DA_TQ), jnp.float32),
        ],
    )
    return pl.pallas_call(
        functools.partial(_diff_attn_kernel, nkc=nkc),
        grid_spec=grid_spec,
        out_shape=jax.ShapeDtypeStruct((DA_HEADS, m, LANES), jnp.bfloat16),
        compiler_params=pltpu.CompilerParams(
            dimension_semantics=("arbitrary", "arbitrary", "arbitrary"),
            vmem_limit_bytes=VMEM_LIMIT),
        name="diff_attn",
    )(s2_used, bounds, refpt, proj_t, proj_n, proj_t, k_tab, q_aug, lam_vecs, subln_col)


NA_GROUP = 32
NA_TOK = NA_GROUP * GRID_W
NA_WIN = NA_ROWS * GRID_W
NA_FIXED_REACH = 64.0
NA_AHEAD = 2
NA_BIAS_ROWS = (2 * NA_ROWS - 1) * GRID_W


def _nbr_attn_kernel(ok_ref, refpt_ref, q_ref, k_ref, v_ref, bias_ref, o_ref, *, rows):
    g = pl.program_id(2)
    step = (pl.program_id(0) * NA_PAIRS + pl.program_id(1)) * pl.num_programs(2) + g
    lane_q = lax.broadcasted_iota(jnp.int32, (GRID_W, LANES), 1)
    dn_last = (((1,), (1,)), ((), ()))
    dn_first = (((0,), (0,)), ((), ()))
    ones = jnp.ones((NA_WIN, LANES), jnp.bfloat16)

    def key_start(rr):
        r = g * NA_GROUP + rr
        rs = jnp.clip(r - NA_ROWS // 2, 0, rows - NA_ROWS)
        return r, rs, pl.multiple_of(rs * GRID_W, GRID_W)

    def scores(rr):
        r, rs, k0 = key_start(rr)
        b0 = pl.multiple_of((NA_ROWS - 1 - (r - rs)) * GRID_W, GRID_W)
        q = q_ref[rr * GRID_W:(rr + 1) * GRID_W, :]
        zero = jnp.zeros_like(q)
        qbd = jnp.concatenate([jnp.where(lane_q < HEAD_DIM, q, zero),
                               jnp.where(lane_q >= HEAD_DIM, q, zero)], axis=0)
        return lax.dot_general(k_ref[pl.ds(k0, NA_WIN), :], qbd, dn_last,
                               preferred_element_type=jnp.float32) + bias_ref[pl.ds(b0, NA_WIN), :]

    def group(fixed_ref):
        if fixed_ref:
            lane_k = lax.broadcasted_iota(jnp.int32, (1, LANES), 1)
            ref_row = jnp.where(lane_k < HEAD_DIM, refpt_ref[step * 2], refpt_ref[step * 2 + 1])
        ahead = [scores(rr) for rr in range(NA_AHEAD)]
        for rr in range(NA_GROUP):
            st = ahead.pop(0)
            if rr + NA_AHEAD < NA_GROUP:
                ahead.append(scores(rr + NA_AHEAD))
            p = jnp.exp2(st - (ref_row if fixed_ref else jnp.max(st, axis=0, keepdims=True)))
            v_ext = jnp.concatenate([v_ref[pl.ds(key_start(rr)[2], NA_WIN), :], ones], axis=1)
            full = lax.dot_general(p.astype(jnp.bfloat16), v_ext, dn_first,
                                   preferred_element_type=jnp.float32)
            num = jnp.where(lane_q < HEAD_DIM, full[:GRID_W, :LANES], full[GRID_W:, :LANES])
            den = jnp.where(lane_q < HEAD_DIM, full[:GRID_W, LANES:], full[GRID_W:, LANES:])
            o_ref[rr * GRID_W:(rr + 1) * GRID_W, :] = (num / den).astype(o_ref.dtype)

    for fixed in (True, False):
        pl.when((ok_ref[step] == 1) if fixed else (ok_ref[step] != 1))(functools.partial(group, fixed))


def _nbr_ref_points(stats, rpb_l, batch, seq):
    ng = seq // NA_TOK
    nt = seq // IN_TM
    assert NA_ROWS * GRID_W <= IN_TM, "a query's window must stay inside its own and the adjacent tiles"
    norms = DA_NORM_MARGIN * jnp.sqrt(stats[:, 2:4, :NA_HEADS]).reshape(batch, nt, 2, NA_HEADS)
    kn, qn = norms[:, :, 0], norms[:, :, 1]
    kn_prev = jnp.concatenate([kn[:, :1], kn[:, :-1]], axis=1)
    kn_next = jnp.concatenate([kn[:, 1:], kn[:, -1:]], axis=1)
    kn_win = jnp.maximum(jnp.maximum(kn_prev, kn), kn_next)
    kn_win = kn_win.reshape(batch, ng, nt // ng, NA_HEADS).max(axis=2)
    qn = qn.reshape(batch, ng, nt // ng, NA_HEADS).max(axis=2)
    bias = rpb_l.astype(jnp.float32) * LOG2E
    b_self = bias[:, NA_ROWS - 1, NA_COLS - 1]
    b_max = jnp.max(bias, axis=(1, 2))
    refpt = -(qn * kn_win) + b_self
    spread = 2.0 * qn * kn_win + (b_max - b_self)
    ok = jnp.all((spread <= NA_FIXED_REACH).reshape(batch, ng, NA_PAIRS, 2), axis=-1)
    refpt = jnp.transpose(refpt.reshape(batch, ng, NA_PAIRS, 2), (0, 2, 1, 3))
    return (jnp.transpose(ok, (0, 2, 1)).reshape(-1).astype(jnp.int32),
            refpt.reshape(-1).astype(jnp.float32))


def _nbr_attn(proj, stats, rpb_l, batch, seq):
    rows = seq // GRID_W
    ng = seq // NA_TOK
    m = batch * seq
    ok, refpt = _nbr_ref_points(stats, rpb_l, batch, seq)
    grid_spec = pltpu.PrefetchScalarGridSpec(
        num_scalar_prefetch=2,
        grid=(batch, NA_PAIRS, ng),
        in_specs=[
            pl.BlockSpec((None, NA_TOK, LANES), lambda b, hp, g, ok, rp: (CB_QB + hp, b * ng + g, 0)),
            pl.BlockSpec((None, seq, LANES), lambda b, hp, g, ok, rp: (CB_KB + hp, b, 0)),
            pl.BlockSpec((None, seq, LANES), lambda b, hp, g, ok, rp: (CB_VB + hp, b, 0)),
            pl.BlockSpec((None, NA_BIAS_ROWS, LANES), lambda b, hp, g, ok, rp: (hp, 0, 0)),
        ],
        out_specs=pl.BlockSpec((None, NA_TOK, LANES), lambda b, hp, g, ok, rp: (hp, b * ng + g, 0)),
    )
    return pl.pallas_call(
        functools.partial(_nbr_attn_kernel, rows=rows),
        grid_spec=grid_spec,
        out_shape=jax.ShapeDtypeStruct((NA_PAIRS, m, LANES), jnp.bfloat16),
        compiler_params=pltpu.CompilerParams(
            dimension_semantics=("arbitrary", "arbitrary", "arbitrary"),
            vmem_limit_bytes=VMEM_LIMIT),
        name="nbr_attn",
    )(ok, refpt, proj, proj, proj, _nbr_bias_table(rpb_l))


def _nbr_bias_table(rpb_l):
    col = np.arange(GRID_W)
    col_start = np.clip(col - NA_COLS // 2, 0, GRID_W - NA_COLS)
    col_in = (col[None, :] >= col_start[:, None]) & (col[None, :] < col_start[:, None] + NA_COLS)
    col_off = np.clip(col[None, :] - col[:, None] + NA_COLS - 1, 0, 2 * NA_COLS - 2)
    ncol = 2 * NA_COLS - 1
    pick = (col_off.T[None] == np.arange(ncol)[:, None, None]).astype(np.float32)
    pick2 = np.zeros((2, ncol, GRID_W, 2, GRID_W), np.float32)
    for hh in range(2):
        pick2[hh, :, :, hh, :] = pick
    rp = rpb_l.astype(jnp.float32).reshape(NA_PAIRS, 2, 2 * NA_ROWS - 1, ncol)
    t = jnp.einsum('pjic,jckhq->pikhq', rp, pick2, precision=lax.Precision.HIGHEST)
    t = jnp.where(col_in.T[None, None, :, None, :], t * LOG2E, NEG)
    return t.reshape(NA_PAIRS, NA_BIAS_ROWS, LANES)


MIX_TM = 1024


def _cat_blocks(ref):
    return jnp.concatenate([ref[c] for c in range(ref.shape[0])], axis=-1)


def _mix_out_kernel(x_ref, ya_ref, yb_ref, ga_ref, gb_ref, wa_ref, wb_ref, wo_ref, nw_ref, o_ref):
    ya = _cat_blocks(ya_ref)
    yb = _cat_blocks(yb_ref)
    ga = _cat_blocks(ga_ref).astype(jnp.float32)
    gb = _cat_blocks(gb_ref).astype(jnp.float32)
    bf = jnp.bfloat16
    merged = (ga * jnp.dot(ya, wa_ref[...].astype(bf), preferred_element_type=jnp.float32)
              + gb * jnp.dot(yb, wb_ref[...].astype(bf), preferred_element_type=jnp.float32))
    t = jnp.dot(merged.astype(bf), wo_ref[...].astype(bf), preferred_element_type=jnp.float32)
    o_ref[...] = x_ref[...] + _rms(t, nw_ref[...])


def _mix_out(x2, ya, yb, proj, wa, wb, wo, nw):
    m = x2.shape[0]
    const = lambda i: (0, 0)
    gate_blk = D_MODEL // LANES
    return pl.pallas_call(
        _mix_out_kernel,
        grid=(m // MIX_TM,),
        in_specs=[
            pl.BlockSpec((MIX_TM, D_MODEL), lambda i: (i, 0)),
            pl.BlockSpec((DA_HEADS, MIX_TM, LANES), lambda i: (0, i, 0)),
            pl.BlockSpec((NA_PAIRS, MIX_TM, LANES), lambda i: (0, i, 0)),
            pl.BlockSpec((gate_blk, MIX_TM, LANES), lambda i: (CB_GA // gate_blk, i, 0)),
            pl.BlockSpec((gate_blk, MIX_TM, LANES), lambda i: (CB_GB // gate_blk, i, 0)),
            pl.BlockSpec(wa.shape, const),
            pl.BlockSpec(wb.shape, const),
            pl.BlockSpec(wo.shape, const),
            pl.BlockSpec((1, D_MODEL), const),
        ],
        out_specs=pl.BlockSpec((MIX_TM, D_MODEL), lambda i: (i, 0)),
        out_shape=jax.ShapeDtypeStruct((m, D_MODEL), jnp.float32),
        compiler_params=pltpu.CompilerParams(
            dimension_semantics=("arbitrary",),
            vmem_limit_bytes=VMEM_LIMIT),
        name="mix_out",
    )(x2, ya, yb, proj, proj, wa, wb, wo, nw)


FFN_TM = 512
FFN_CHUNK = 256
FFN_NCHUNK = D_FF // FFN_CHUNK


def _ffn_kernel(x_ref, nw1_ref, wg_ref, wu_ref, wd_ref, nw2_ref, o_ref):
    x = x_ref[...]
    h = _rms(x, nw1_ref[...]).astype(jnp.bfloat16)
    f = jnp.zeros((FFN_TM, D_MODEL), jnp.float32)
    for c in range(FFN_NCHUNK):
        sl = slice(c * FFN_CHUNK, (c + 1) * FFN_CHUNK)
        g = jnp.dot(h, wg_ref[:, sl], preferred_element_type=jnp.float32)
        u = jnp.dot(h, wu_ref[:, sl], preferred_element_type=jnp.float32)
        a = (jax.nn.silu(g) * u).astype(jnp.bfloat16)
        f = f + jnp.dot(a, wd_ref[sl, :], preferred_element_type=jnp.float32)
    o_ref[...] = x + _rms(f, nw2_ref[...])


def _ffn(x1, nw1, wg, wu, wd, nw2):
    m = x1.shape[0]
    const = lambda i: (0, 0)
    once = pl.Buffered(1)
    return pl.pallas_call(
        _ffn_kernel,
        grid=(m // FFN_TM,),
        in_specs=[
            pl.BlockSpec((FFN_TM, D_MODEL), lambda i: (i, 0)),
            pl.BlockSpec((1, D_MODEL), const),
            pl.BlockSpec(wg.shape, const, pipeline_mode=once),
            pl.BlockSpec(wu.shape, const, pipeline_mode=once),
            pl.BlockSpec(wd.shape, const, pipeline_mode=once),
            pl.BlockSpec((1, D_MODEL), const),
        ],
        out_specs=pl.BlockSpec((FFN_TM, D_MODEL), lambda i: (i, 0)),
        out_shape=jax.ShapeDtypeStruct((m, D_MODEL), jnp.float32),
        compiler_params=pltpu.CompilerParams(
            dimension_semantics=("arbitrary",),
            vmem_limit_bytes=VMEM_LIMIT),
        name="ffn",
    )(x1, nw1, wg, wu, wd, nw2)


def kernel(x, pre_mix_w, w_in, b_gate, lambda_q1, lambda_k1, lambda_q2, lambda_k2, subln_w, rpb,
           w_branch_a, w_branch_b, w_out, post_mix_w, pre_ffn_w, w_gate, w_up, w_down, post_ffn_w):
    batch, seq, _ = x.shape
    depth = w_in.shape[0]
    assert depth == 1, "LAMBDA_INIT is specialised to a single layer"
    bf = jnp.bfloat16
    f32 = jnp.float32
    slopes = np.asarray([2.0 ** (-8.0 * (i + 1) / DA_HEADS) for i in range(DA_HEADS)], np.float32)
    grp = (jnp.arange(DA_W)[:, None] // HEAD_DIM == jnp.arange(LANES)[None, :]).astype(bf)
    x2 = x.reshape(batch * seq, D_MODEL)
    for l in range(depth):
        proj_t, proj_n, stats = _in_proj(x2, pre_mix_w[l].reshape(1, D_MODEL).astype(f32), w_in[l].astype(f32),
                                         b_gate[l].reshape(1, 2 * D_MODEL).astype(f32), grp)
        lam_vecs = jnp.stack([lambda_q1[l], lambda_k1[l], lambda_q2[l], lambda_k2[l]]).astype(f32)
        ya = _diff_attn(proj_t, proj_n, stats, slopes, lam_vecs,
                        subln_w[l].reshape(2 * HEAD_DIM, 1).astype(f32), batch, seq)
        yb = _nbr_attn(proj_n, stats, rpb[l], batch, seq)
        x1 = _mix_out(x2, ya, yb, proj_n, w_branch_a[l].astype(f32), w_branch_b[l].astype(f32),
                      w_out[l].astype(f32), post_mix_w[l].reshape(1, D_MODEL).astype(f32))
        x2 = _ffn(x1, pre_ffn_w[l].reshape(1, D_MODEL).astype(f32), w_gate[l].astype(bf),
                  w_up[l].astype(bf), w_down[l].astype(bf), post_ffn_w[l].reshape(1, D_MODEL).astype(f32))
    return x2.reshape(batch, seq, D_MODEL)
```

```python
import functools
import math

import jax
import jax.numpy as jnp
import numpy as np
from jax import lax
from jax.experimental import pallas as pl
from jax.experimental.pallas import tpu as pltpu

D_MODEL = 1024
HEAD_DIM = 64
DA_HEADS = 4
DA_W = DA_HEADS * 2 * HEAD_DIM
NA_HEADS = 8
NA_PAIRS = NA_HEADS // 2
NA_W = NA_HEADS * HEAD_DIM
GRID_W = 64
NA_ROWS = 8
NA_COLS = 16
D_FF = 2816
IN_COLS = 5120
EPS = 1e-6
NEG = -1e30
LANES = 128
LOG2E = math.log2(math.e)
LAMBDA_INIT = 0.8 - 0.6 * math.exp(-0.3 * 0)

W_QA, W_KA, W_VA = 0, DA_W, 2 * DA_W
W_QB, W_KB = 3 * DA_W, 3 * DA_W + NA_W
W_GATES = 3 * DA_W + 3 * NA_W
CB_KA, CB_QB, CB_KB, CB_VB, CB_GA, CB_GB = 0, 4, 8, 12, 16, 24
N_COLBLK_N = 32
RB_QA, RB_VA = 0, 4

VMEM_LIMIT = 56 * 1024 * 1024


def _rms(xf, w):
    return xf * lax.rsqrt(jnp.mean(xf * xf, axis=-1, keepdims=True) + EPS) * w


IN_TM = 512
IN_TN = 1024
IN_NBLK = 2 * DA_W // LANES
Q_SCALE = HEAD_DIM ** -0.5 * LOG2E
STAT_ROWS = 8


def _max_group_sqnorm(a, grp):
    ab = a.astype(jnp.bfloat16).astype(jnp.float32)
    n2 = jnp.dot((ab * ab).astype(jnp.bfloat16), grp, preferred_element_type=jnp.float32)
    return jnp.max(n2, axis=0, keepdims=True)


def _in_proj_kernel(x_ref, nw_ref, w_ref, b_ref, grp_ref, ot_ref, on_ref, st_ref, wb_ref):
    @pl.when(pl.program_id(0) == 0)
    def _():
        for c in range(IN_COLS // IN_TN):
            wb_ref[:, c * IN_TN:(c + 1) * IN_TN] = w_ref[:, c * IN_TN:(c + 1) * IN_TN].astype(wb_ref.dtype)

    h = _rms(x_ref[...], nw_ref[...]).astype(jnp.bfloat16)

    def proj(col0, ncols):
        return jnp.dot(h, wb_ref[:, col0:col0 + ncols], preferred_element_type=jnp.float32)

    def store_natural(blk0, val):
        for j in range(val.shape[1] // LANES):
            on_ref[blk0 + j] = val[:, j * LANES:(j + 1) * LANES].astype(on_ref.dtype)

    def store_transposed(blk0, val):
        for j in range(val.shape[1] // LANES):
            ot_ref[blk0 + j, 0] = val[:, j * LANES:(j + 1) * LANES].T.astype(ot_ref.dtype)

    grp = grp_ref[...]
    qa = proj(W_QA, DA_W) * Q_SCALE
    store_transposed(RB_QA, qa)
    q_stat = _max_group_sqnorm(qa, grp)
    store_transposed(RB_VA, proj(W_VA, DA_W))
    ka = proj(W_KA, DA_W)
    store_natural(CB_KA, ka)
    k_stat = _max_group_sqnorm(ka, grp)
    qb = proj(W_QB, NA_W) * Q_SCALE
    store_natural(CB_QB, qb)
    qb_stat = _max_group_sqnorm(qb, grp)
    kv = proj(W_KB, 2 * NA_W)
    store_natural(CB_KB, kv)
    kb_stat = _max_group_sqnorm(kv[:, :NA_W], grp)
    for half in range(2):
        cols = slice(half * D_MODEL, (half + 1) * D_MODEL)
        store_natural(CB_GA + half * (D_MODEL // LANES),
                      jax.nn.sigmoid(proj(W_GATES + half * D_MODEL, D_MODEL) + b_ref[:, cols]))
    st_ref[0] = jnp.concatenate(
        [k_stat, q_stat, kb_stat, qb_stat, jnp.zeros((STAT_ROWS - 4, LANES), jnp.float32)], axis=0)


def _in_proj(x2, nw, w_f32, b_gate_row, grp):
    m = x2.shape[0]
    nt = m // IN_TM
    const = lambda i: (0, 0)
    return pl.pallas_call(
        _in_proj_kernel,
        grid=(nt,),
        in_specs=[
            pl.BlockSpec((IN_TM, D_MODEL), lambda i: (i, 0)),
            pl.BlockSpec((1, D_MODEL), const),
            pl.BlockSpec((D_MODEL, IN_COLS), const, pipeline_mode=pl.Buffered(1)),
            pl.BlockSpec((1, 2 * D_MODEL), const),
            pl.BlockSpec((DA_W, LANES), const),
        ],
        out_specs=[
            pl.BlockSpec((IN_NBLK, 1, LANES, IN_TM), lambda i: (0, i, 0, 0)),
            pl.BlockSpec((N_COLBLK_N, IN_TM, LANES), lambda i: (0, i, 0)),
            pl.BlockSpec((1, STAT_ROWS, LANES), lambda i: (i, 0, 0)),
        ],
        out_shape=[
            jax.ShapeDtypeStruct((IN_NBLK, nt, LANES, IN_TM), jnp.bfloat16),
            jax.ShapeDtypeStruct((N_COLBLK_N, m, LANES), jnp.bfloat16),
            jax.ShapeDtypeStruct((nt, STAT_ROWS, LANES), jnp.float32),
        ],
        scratch_shapes=[pltpu.VMEM((D_MODEL, IN_COLS), jnp.bfloat16)],
        compiler_params=pltpu.CompilerParams(
            dimension_semantics=("arbitrary",),
            vmem_limit_bytes=VMEM_LIMIT),
        name="in_proj",
    )(x2, nw, w_f32, b_gate_row, grp)


DA_TQ = 1024
DA_TK = 512
DA_QG = 256
DA_NG = DA_TQ // DA_QG
DA_QT = DA_TQ // IN_TM
DA_DIAG = DA_TQ // DA_TK
DA_NAUG = 9
BF16_EXACT_INT = 256
DA_AHEAD = 4
DA_SKIP_LOG2 = 127.0
DA_FIXED_REACH = 64.0
DA_NBOUND = 5
DA_NEAR_LEFT = ((1, 0), (1, 1))
DA_NEAR_RIGHT = ((0, 3), (0, 2))
DA_NORM_MARGIN = 1.01


def _alibi_tables(slopes):
    f32, bf = np.float32, jnp.bfloat16
    s2 = slopes.astype(f32) * f32(LOG2E)
    c1 = s2.astype(bf)
    c2 = (s2 - c1.astype(f32)).astype(bf)
    c3 = (s2 - c1.astype(f32) - c2.astype(f32)).astype(bf)
    parts = np.stack([c1, c2, c3], axis=1).astype(f32)
    s2_used = parts[:, 0] + parts[:, 1] + parts[:, 2]
    dk = np.arange(DA_TK)
    dk_lo = (dk % BF16_EXACT_INT).astype(f32)
    dk_hi = (dk - dk % BF16_EXACT_INT).astype(f32)
    ones3 = np.ones((3,), f32)
    k_aug = np.concatenate([
        np.broadcast_to((dk_lo[:, None] * ones3)[None], (DA_HEADS, DA_TK, 3)),
        np.broadcast_to((dk_hi[:, None] * ones3)[None], (DA_HEADS, DA_TK, 3)),
        np.broadcast_to(parts[:, None, :], (DA_HEADS, DA_TK, 3)),
    ], axis=-1)
    pad_k = np.zeros((DA_HEADS, DA_TK, HEAD_DIM - DA_NAUG), f32)
    k_aug = np.concatenate([k_aug, pad_k], axis=-1)
    zeros_k = np.zeros((DA_HEADS, DA_TK, HEAD_DIM), f32)
    k_tab = np.stack([np.concatenate([zeros_k, k_aug], axis=-1),
                      np.concatenate([k_aug, zeros_k], axis=-1)], axis=1)
    dq = (np.arange(DA_TQ) % DA_QG).astype(f32)
    sig = np.asarray([1.0, -1.0], f32)
    q_par = sig[None, :, None, None] * parts[:, None, :, None] * np.ones((DA_TQ,), f32)
    q_off = -sig[None, :, None, None] * np.broadcast_to(dq, (DA_HEADS, 1, 3, DA_TQ))
    q_aug = np.concatenate([q_par, q_par, q_off], axis=2)
    q_aug = np.concatenate(
        [q_aug, np.zeros((DA_HEADS, 2, HEAD_DIM - DA_NAUG, DA_TQ), f32)], axis=2)
    return k_tab.astype(bf), q_aug.astype(bf), s2_used


def _diff_attn_kernel(s2_ref, bounds_ref, refpt_ref, q_ref, k_ref, v_ref, ktab_ref, qaug_ref, lam_ref, sw_ref, o_ref,
                      ka_ref, qm_ref, dist_ref, m_ref, l_ref, acc_ref, *, nkc):
    h = pl.program_id(1)
    qi = pl.program_id(2)
    slope2 = s2_ref[h]

    @pl.when(qi == 0)
    def _():
        lane = lax.broadcasted_iota(jnp.int32, (DA_TK, LANES), 1)
        for t in range(nkc):
            rows = slice(t * DA_TK, (t + 1) * DA_TK)
            kt = k_ref[rows, :]
            ka_ref[0, rows, :] = jnp.where(lane < HEAD_DIM, kt, ktab_ref[0])
            ka_ref[1, rows, :] = jnp.where(lane >= HEAD_DIM, kt, ktab_ref[1])
        dq = lax.broadcasted_iota(jnp.int32, (DA_TK, DA_QG), 1)
        dk = lax.broadcasted_iota(jnp.int32, (DA_TK, DA_QG), 0)
        dist_ref[...] = (dq - dk).astype(jnp.float32) * slope2

    for t in range(DA_QT):
        qt = q_ref[t]
        cols = slice(t * IN_TM, (t + 1) * IN_TM)
        for sg in range(2):
            qm_ref[sg, 0, :HEAD_DIM, cols] = qt[:HEAD_DIM]
            qm_ref[sg, 0, HEAD_DIM:, cols] = qaug_ref[sg, :, cols]
            qm_ref[sg, 1, :HEAD_DIM, cols] = qaug_ref[sg, :, cols]
            qm_ref[sg, 1, HEAD_DIM:, cols] = qt[HEAD_DIM:]
    m_ref[...] = jnp.full_like(m_ref, -jnp.inf)
    l_ref[...] = jnp.zeros_like(l_ref)
    acc_ref[...] = jnp.zeros_like(acc_ref)

    step = (pl.program_id(0) * DA_HEADS + h) * pl.num_programs(2) + qi

    def make_body(sg, diagonal, fixed_ref=False, positions=None, nblocks=1):
        tiles = [(sub, g, mp) for sub in range(DA_DIAG * nblocks) for g in range(DA_NG) for mp in range(2)
                 if positions is None or (sub, g) in positions]

        def side(sub, g):
            if not diagonal:
                return sg, (1.0 if sg == 0 else -1.0), False
            c = g * DA_QG - sub * DA_TK
            if c >= DA_TK:
                return 0, 1.0, False
            if c <= -DA_QG:
                return 1, -1.0, False
            return 0, 1.0, True

        def body(kb, carry):
            def origin(sub, g):
                kc = kb * DA_DIAG + sub
                return (qi * DA_TQ + g * DA_QG - kc * DA_TK).astype(jnp.float32) * slope2

            corrections = {}

            def correction(sub, g):
                if (sub, g) not in corrections:
                    corrections[sub, g] = 2.0 * jnp.minimum(dist_ref[...] + origin(sub, g), 0.0)
                return corrections[sub, g]

            def scores(sub, g, mp):
                cols = slice(g * DA_QG, (g + 1) * DA_QG)
                k0 = pl.multiple_of((kb * DA_DIAG + sub) * DA_TK, DA_TK)
                variant, _, straddles = side(sub, g)
                st = jnp.dot(ka_ref[mp, pl.ds(k0, DA_TK), :], qm_ref[variant, mp, :, cols],
                             preferred_element_type=jnp.float32)
                return st + correction(sub, g) if straddles else st

            ahead = [scores(*tiles[i]) for i in range(min(DA_AHEAD, len(tiles)))]
            for ti, (sub, g, mp) in enumerate(tiles):
                cols = slice(g * DA_QG, (g + 1) * DA_QG)
                st = ahead.pop(0)
                if ti + DA_AHEAD < len(tiles):
                    ahead.append(scores(*tiles[ti + DA_AHEAD]))
                shift = side(sub, g)[1] * origin(sub, g)
                vt = v_ref[kb * DA_DIAG + sub]
                if fixed_ref:
                    p = jnp.exp2(st - (refpt_ref[step * 2 + mp] + shift))
                    l_ref[mp, :, cols] = l_ref[mp, :, cols] + jnp.sum(p, axis=0, keepdims=True)
                    acc_ref[mp, :, cols] = acc_ref[mp, :, cols] + jnp.dot(
                        vt, p.astype(jnp.bfloat16), preferred_element_type=jnp.float32)
                else:
                    m_old = m_ref[mp, :, cols]
                    m_new = jnp.maximum(m_old, jnp.max(st, axis=0, keepdims=True) - shift)
                    alpha = jnp.exp2(m_old - m_new)
                    p = jnp.exp2(st - (m_new + shift))
                    l_ref[mp, :, cols] = alpha * l_ref[mp, :, cols] + jnp.sum(p, axis=0, keepdims=True)
                    acc_ref[mp, :, cols] = alpha * acc_ref[mp, :, cols] + jnp.dot(
                        vt, p.astype(jnp.bfloat16), preferred_element_type=jnp.float32)
                    m_ref[mp, :, cols] = m_new
            return carry

        return body

    lo = bounds_ref[step * DA_NBOUND]
    hi = bounds_ref[step * DA_NBOUND + 1]
    fixed_ok = bounds_ref[step * DA_NBOUND + 2]
    near_l = bounds_ref[step * DA_NBOUND + 3]
    near_r = bounds_ref[step * DA_NBOUND + 4]

    @pl.when(fixed_ok == 1)
    def _():
        make_body(0, True, True)(qi, 0)
        for sg, first, count in ((0, lo, qi - lo), (1, qi + 1, hi - qi - 1)):
            odd = count % 2
            single = make_body(sg, False, True)
            pair = make_body(sg, False, True, nblocks=2)
            lax.fori_loop(first, first + odd, single, 0)
            lax.fori_loop(0, count // 2, lambda j, c, pair=pair, start=first + odd: pair(start + 2 * j, c), 0)

        @pl.when(near_l == 1)
        def _():
            make_body(0, False, True, DA_NEAR_LEFT)(lo - 1, 0)

        @pl.when(near_r == 1)
        def _():
            make_body(1, False, True, DA_NEAR_RIGHT)(hi, 0)

    @pl.when(fixed_ok != 1)
    def _():
        make_body(0, True)(qi, 0)
        lax.fori_loop(lo - near_l, qi, make_body(0, False), 0)
        lax.fori_loop(qi + 1, hi + near_r, make_body(1, False), 0)

    lv = lam_ref[...]
    lam = (jnp.exp(jnp.sum(lv[0:1] * lv[1:2], axis=-1, keepdims=True))
           - jnp.exp(jnp.sum(lv[2:3] * lv[3:4], axis=-1, keepdims=True))
           + LAMBDA_INIT)
    o = acc_ref[0] / l_ref[0] - lam * (acc_ref[1] / l_ref[1])
    o = o * lax.rsqrt(jnp.mean(o * o, axis=0, keepdims=True) + EPS) * sw_ref[...] * (1.0 - LAMBDA_INIT)
    o_ref[...] = o.T.astype(o_ref.dtype)


def _skip_bounds(stats, s2_used, batch, seq):
    nq, nkc = seq // DA_TQ, seq // DA_TK
    nkb = nkc // DA_DIAG
    norms = DA_NORM_MARGIN * jnp.sqrt(stats[:, :2, :2 * DA_HEADS])
    norms = norms.reshape(batch, nkc, 2, DA_HEADS, 2)
    kn = norms[:, :, 0]
    qn = norms[:, :, 1].reshape(batch, nq, DA_QT, DA_HEADS, 2).max(axis=2)
    kn_self = kn.reshape(batch, nq, DA_DIAG, DA_HEADS, 2).max(axis=2)
    reach = qn[:, :, None] * (kn[:, None, :] + kn_self[:, :, None])
    q_lo = (np.arange(nq)[:, None] * DA_TQ + np.arange(DA_NG)[None, :] * DA_QG)[:, :, None]
    k_lo = (np.arange(nkc) * DA_TK)[None, None, :]
    gap = np.maximum(np.maximum(q_lo - (k_lo + DA_TK - 1), k_lo - (q_lo + DA_QG - 1)), 0)
    bound = (reach[:, :, None] - s2_used[None, None, None, None, :, None]
             * gap[None, :, :, :, None, None].astype(np.float32))
    need = jnp.any(~(bound < -DA_SKIP_LOG2), axis=-1) | (gap == 0)[None, :, :, :, None]
    need = need.reshape(batch, nq, DA_NG, nkb, DA_DIAG, DA_HEADS)
    near = np.zeros((2, DA_NG, DA_DIAG), bool)
    for side, positions in enumerate((DA_NEAR_LEFT, DA_NEAR_RIGHT)):
        for sub, g in positions:
            near[side, g, sub] = True
    blk = np.arange(nkb)
    left = (blk[None, :] < np.arange(nq)[:, None])[:, None, :, None]
    right = (blk[None, :] > np.arange(nq)[:, None])[:, None, :, None]
    in_near = (left & near[0][None, :, None, :]) | (right & near[1][None, :, None, :])
    need_any = need.any(axis=(2, 4))
    need_far = (need & ~in_near[None, :, :, :, :, None]).any(axis=(2, 4))
    blk = blk[None, None, :, None]
    lo_far = jnp.min(jnp.where(need_far, blk, nkb), axis=2)
    lo_any = jnp.min(jnp.where(need_any, blk, nkb), axis=2)
    hi_far = jnp.max(jnp.where(need_far, blk + 1, 0), axis=2)
    hi_any = jnp.max(jnp.where(need_any, blk + 1, 0), axis=2)
    near_l = lo_any == lo_far - 1
    near_r = hi_any == hi_far + 1
    lo = jnp.where(lo_any >= lo_far - 1, lo_far, lo_any)
    hi = jnp.where(hi_any <= hi_far + 1, hi_far, hi_any)
    refpt = -(qn * kn_self)
    reach_blk = reach.reshape(batch, nq, nkb, DA_DIAG, DA_HEADS, 2).max(axis=(3, 5))
    visited = (blk >= jnp.minimum(lo, lo_any)[:, :, None]) & (blk < jnp.maximum(hi, hi_any)[:, :, None])
    fixed_ok = jnp.all(~visited | (reach_blk <= DA_FIXED_REACH), axis=2)
    bounds = jnp.stack([lo, hi, fixed_ok.astype(lo.dtype), near_l.astype(lo.dtype),
                        near_r.astype(lo.dtype)], axis=-1)
    return (jnp.transpose(bounds, (0, 2, 1, 3)).reshape(-1).astype(jnp.int32),
            jnp.transpose(refpt, (0, 2, 1, 3)).reshape(-1).astype(jnp.float32))


def _diff_attn(proj_t, proj_n, stats, slopes, lam_vecs, subln_col, batch, seq):
    nq = seq // DA_TQ
    nkc = seq // DA_TK
    m = batch * seq
    k_tab, q_aug, s2_used = _alibi_tables(slopes)
    bounds, refpt = _skip_bounds(stats, s2_used, batch, seq)
    grid_spec = pltpu.PrefetchScalarGridSpec(
        num_scalar_prefetch=3,
        grid=(batch, DA_HEADS, nq),
        in_specs=[
            pl.BlockSpec((None, DA_QT, LANES, IN_TM), lambda b, h, qi, s, bd, rp: (RB_QA + h, b * nq + qi, 0, 0)),
            pl.BlockSpec((None, seq, LANES), lambda b, h, qi, s, bd, rp: (CB_KA + h, b, 0)),
            pl.BlockSpec((None, nkc, LANES, DA_TK), lambda b, h, qi, s, bd, rp: (RB_VA + h, b, 0, 0)),
            pl.BlockSpec((None, 2, DA_TK, LANES), lambda b, h, qi, s, bd, rp: (h, 0, 0, 0)),
            pl.BlockSpec((None, 2, HEAD_DIM, DA_TQ), lambda b, h, qi, s, bd, rp: (h, 0, 0, 0)),
            pl.BlockSpec((4, HEAD_DIM), lambda b, h, qi, s, bd, rp: (0, 0)),
            pl.BlockSpec((2 * HEAD_DIM, 1), lambda b, h, qi, s, bd, rp: (0, 0)),
        ],
        out_specs=pl.BlockSpec((None, DA_TQ, LANES), lambda b, h, qi, s, bd, rp: (h, b * nq + qi, 0)),
        scratch_shapes=[
            pltpu.VMEM((2, seq, LANES), jnp.bfloat16),
            pltpu.VMEM((2, 2, LANES, DA_TQ), jnp.bfloat16),
            pltpu.VMEM((DA_TK, DA_QG), jnp.float32),
            pltpu.VMEM((2, 1, DA_TQ), jnp.float32),
            pltpu.VMEM((2, 1, DA_TQ), jnp.float32),
            pltpu.VMEM((2, LANES, DA_TQ), jnp.float32),
        ],
    )
    return pl.pallas_call(
        functools.partial(_diff_attn_kernel, nkc=nkc),
        grid_spec=grid_spec,
        out_shape=jax.ShapeDtypeStruct((DA_HEADS, m, LANES), jnp.bfloat16),
        compiler_params=pltpu.CompilerParams(
            dimension_semantics=("arbitrary", "arbitrary", "arbitrary"),
            vmem_limit_bytes=VMEM_LIMIT),
        name="diff_attn",
    )(s2_used, bounds, refpt, proj_t, proj_n, proj_t, k_tab, q_aug, lam_vecs, subln_col)


NA_GROUP = 32
NA_TOK = NA_GROUP * GRID_W
NA_WIN = NA_ROWS * GRID_W
NA_FIXED_REACH = 64.0
NA_AHEAD = 2
NA_BIAS_ROWS = (2 * NA_ROWS - 1) * GRID_W


def _nbr_attn_kernel(ok_ref, refpt_ref, q_ref, k_ref, v_ref, bias_ref, o_ref, *, rows):
    g = pl.program_id(2)
    step = (pl.program_id(0) * NA_PAIRS + pl.program_id(1)) * pl.num_programs(2) + g
    lane_q = lax.broadcasted_iota(jnp.int32, (GRID_W, LANES), 1)
    dn_last = (((1,), (1,)), ((), ()))
    dn_first = (((0,), (0,)), ((), ()))
    ones = jnp.ones((NA_WIN, LANES), jnp.bfloat16)

    def key_start(rr):
        r = g * NA_GROUP + rr
        rs = jnp.clip(r - NA_ROWS // 2, 0, rows - NA_ROWS)
        return r, rs, pl.multiple_of(rs * GRID_W, GRID_W)

    def scores(rr):
        r, rs, k0 = key_start(rr)
        b0 = pl.multiple_of((NA_ROWS - 1 - (r - rs)) * GRID_W, GRID_W)
        q = q_ref[rr * GRID_W:(rr + 1) * GRID_W, :]
        zero = jnp.zeros_like(q)
        qbd = jnp.concatenate([jnp.where(lane_q < HEAD_DIM, q, zero),
                               jnp.where(lane_q >= HEAD_DIM, q, zero)], axis=0)
        return lax.dot_general(k_ref[pl.ds(k0, NA_WIN), :], qbd, dn_last,
                               preferred_element_type=jnp.float32) + bias_ref[pl.ds(b0, NA_WIN), :]

    def group(fixed_ref):
        if fixed_ref:
            lane_k = lax.broadcasted_iota(jnp.int32, (1, LANES), 1)
            ref_row = jnp.where(lane_k < HEAD_DIM, refpt_ref[step * 2], refpt_ref[step * 2 + 1])
        ahead = [scores(rr) for rr in range(NA_AHEAD)]
        for rr in range(NA_GROUP):
            st = ahead.pop(0)
            if rr + NA_AHEAD < NA_GROUP:
                ahead.append(scores(rr + NA_AHEAD))
            p = jnp.exp2(st - (ref_row if fixed_ref else jnp.max(st, axis=0, keepdims=True)))
            v_ext = jnp.concatenate([v_ref[pl.ds(key_start(rr)[2], NA_WIN), :], ones], axis=1)
            full = lax.dot_general(p.astype(jnp.bfloat16), v_ext, dn_first,
                                   preferred_element_type=jnp.float32)
            num = jnp.where(lane_q < HEAD_DIM, full[:GRID_W, :LANES], full[GRID_W:, :LANES])
            den = jnp.where(lane_q < HEAD_DIM, full[:GRID_W, LANES:], full[GRID_W:, LANES:])
            o_ref[rr * GRID_W:(rr + 1) * GRID_W, :] = (num / den).astype(o_ref.dtype)

    for fixed in (True, False):
        pl.when((ok_ref[step] == 1) if fixed else (ok_ref[step] != 1))(functools.partial(group, fixed))


def _nbr_ref_points(stats, rpb_l, batch, seq):
    ng = seq // NA_TOK
    nt = seq // IN_TM
    assert NA_ROWS * GRID_W <= IN_TM, "a query's window must stay inside its own and the adjacent tiles"
    norms = DA_NORM_MARGIN * jnp.sqrt(stats[:, 2:4, :NA_HEADS]).reshape(batch, nt, 2, NA_HEADS)
    kn, qn = norms[:, :, 0], norms[:, :, 1]
    kn_prev = jnp.concatenate([kn[:, :1], kn[:, :-1]], axis=1)
    kn_next = jnp.concatenate([kn[:, 1:], kn[:, -1:]], axis=1)
    kn_win = jnp.maximum(jnp.maximum(kn_prev, kn), kn_next)
    kn_win = kn_win.reshape(batch, ng, nt // ng, NA_HEADS).max(axis=2)
    qn = qn.reshape(batch, ng, nt // ng, NA_HEADS).max(axis=2)
    bias = rpb_l.astype(jnp.float32) * LOG2E
    b_self = bias[:, NA_ROWS - 1, NA_COLS - 1]
    b_max = jnp.max(bias, axis=(1, 2))
    refpt = -(qn * kn_win) + b_self
    spread = 2.0 * qn * kn_win + (b_max - b_self)
    ok = jnp.all((spread <= NA_FIXED_REACH).reshape(batch, ng, NA_PAIRS, 2), axis=-1)
    refpt = jnp.transpose(refpt.reshape(batch, ng, NA_PAIRS, 2), (0, 2, 1, 3))
    return (jnp.transpose(ok, (0, 2, 1)).reshape(-1).astype(jnp.int32),
            refpt.reshape(-1).astype(jnp.float32))


def _nbr_attn(proj, stats, rpb_l, batch, seq):
    rows = seq // GRID_W
    ng = seq // NA_TOK
    m = batch * seq
    ok, refpt = _nbr_ref_points(stats, rpb_l, batch, seq)
    grid_spec = pltpu.PrefetchScalarGridSpec(
        num_scalar_prefetch=2,
        grid=(batch, NA_PAIRS, ng),
        in_specs=[
            pl.BlockSpec((None, NA_TOK, LANES), lambda b, hp, g, ok, rp: (CB_QB + hp, b * ng + g, 0)),
            pl.BlockSpec((None, seq, LANES), lambda b, hp, g, ok, rp: (CB_KB + hp, b, 0)),
            pl.BlockSpec((None, seq, LANES), lambda b, hp, g, ok, rp: (CB_VB + hp, b, 0)),
            pl.BlockSpec((None, NA_BIAS_ROWS, LANES), lambda b, hp, g, ok, rp: (hp, 0, 0)),
        ],
        out_specs=pl.BlockSpec((None, NA_TOK, LANES), lambda b, hp, g, ok, rp: (hp, b * ng + g, 0)),
    )
    return pl.pallas_call(
        functools.partial(_nbr_attn_kernel, rows=rows),
        grid_spec=grid_spec,
        out_shape=jax.ShapeDtypeStruct((NA_PAIRS, m, LANES), jnp.bfloat16),
        compiler_params=pltpu.CompilerParams(
            dimension_semantics=("arbitrary", "arbitrary", "arbitrary"),
            vmem_limit_bytes=VMEM_LIMIT),
        name="nbr_attn",
    )(ok, refpt, proj, proj, proj, _nbr_bias_table(rpb_l))


def _nbr_bias_table(rpb_l):
    col = np.arange(GRID_W)
    col_start = np.clip(col - NA_COLS // 2, 0, GRID_W - NA_COLS)
    col_in = (col[None, :] >= col_start[:, None]) & (col[None, :] < col_start[:, None] + NA_COLS)
    col_off = np.clip(col[None, :] - col[:, None] + NA_COLS - 1, 0, 2 * NA_COLS - 2)
    ncol = 2 * NA_COLS - 1
    pick = (col_off.T[None] == np.arange(ncol)[:, None, None]).astype(np.float32)
    pick2 = np.zeros((2, ncol, GRID_W, 2, GRID_W), np.float32)
    for hh in range(2):
        pick2[hh, :, :, hh, :] = pick
    rp = rpb_l.astype(jnp.float32).reshape(NA_PAIRS, 2, 2 * NA_ROWS - 1, ncol)
    t = jnp.einsum('pjic,jckhq->pikhq', rp, pick2, precision=lax.Precision.HIGHEST)
    t = jnp.where(col_in.T[None, None, :, None, :], t * LOG2E, NEG)
    return t.reshape(NA_PAIRS, NA_BIAS_ROWS, LANES)


MIX_TM = 1024


def _cat_blocks(ref):
    return jnp.concatenate([ref[c] for c in range(ref.shape[0])], axis=-1)


def _mix_out_kernel(x_ref, ya_ref, yb_ref, ga_ref, gb_ref, wa_ref, wb_ref, wo_ref, nw_ref, o_ref):
    ya = _cat_blocks(ya_ref)
    yb = _cat_blocks(yb_ref)
    ga = _cat_blocks(ga_ref).astype(jnp.float32)
    gb = _cat_blocks(gb_ref).astype(jnp.float32)
    bf = jnp.bfloat16
    merged = (ga * jnp.dot(ya, wa_ref[...].astype(bf), preferred_element_type=jnp.float32)
              + gb * jnp.dot(yb, wb_ref[...].astype(bf), preferred_element_type=jnp.float32))
    t = jnp.dot(merged.astype(bf), wo_ref[...].astype(bf), preferred_element_type=jnp.float32)
    o_ref[...] = x_ref[...] + _rms(t, nw_ref[...])


def _mix_out(x2, ya, yb, proj, wa, wb, wo, nw):
    m = x2.shape[0]
    const = lambda i: (0, 0)
    gate_blk = D_MODEL // LANES
    return pl.pallas_call(
        _mix_out_kernel,
        grid=(m // MIX_TM,),
        in_specs=[
            pl.BlockSpec((MIX_TM, D_MODEL), lambda i: (i, 0)),
            pl.BlockSpec((DA_HEADS, MIX_TM, LANES), lambda i: (0, i, 0)),
            pl.BlockSpec((NA_PAIRS, MIX_TM, LANES), lambda i: (0, i, 0)),
            pl.BlockSpec((gate_blk, MIX_TM, LANES), lambda i: (CB_GA // gate_blk, i, 0)),
            pl.BlockSpec((gate_blk, MIX_TM, LANES), lambda i: (CB_GB // gate_blk, i, 0)),
            pl.BlockSpec(wa.shape, const),
            pl.BlockSpec(wb.shape, const),
            pl.BlockSpec(wo.shape, const),
            pl.BlockSpec((1, D_MODEL), const),
        ],
        out_specs=pl.BlockSpec((MIX_TM, D_MODEL), lambda i: (i, 0)),
        out_shape=jax.ShapeDtypeStruct((m, D_MODEL), jnp.float32),
        compiler_params=pltpu.CompilerParams(
            dimension_semantics=("arbitrary",),
            vmem_limit_bytes=VMEM_LIMIT),
        name="mix_out",
    )(x2, ya, yb, proj, proj, wa, wb, wo, nw)


FFN_TM = 512
FFN_CHUNK = 256
FFN_NCHUNK = D_FF // FFN_CHUNK


def _ffn_kernel(x_ref, nw1_ref, wg_ref, wu_ref, wd_ref, nw2_ref, o_ref):
    x = x_ref[...]
    h = _rms(x, nw1_ref[...]).astype(jnp.bfloat16)
    f = jnp.zeros((FFN_TM, D_MODEL), jnp.float32)
    for c in range(FFN_NCHUNK):
        sl = slice(c * FFN_CHUNK, (c + 1) * FFN_CHUNK)
        g = jnp.dot(h, wg_ref[:, sl], preferred_element_type=jnp.float32)
        u = jnp.dot(h, wu_ref[:, sl], preferred_element_type=jnp.float32)
        a = (jax.nn.silu(g) * u).astype(jnp.bfloat16)
        f = f + jnp.dot(a, wd_ref[sl, :], preferred_element_type=jnp.float32)
    o_ref[...] = x + _rms(f, nw2_ref[...])


def _ffn(x1, nw1, wg, wu, wd, nw2):
    m = x1.shape[0]
    const = lambda i: (0, 0)
    once = pl.Buffered(1)
    return pl.pallas_call(
        _ffn_kernel,
        grid=(m // FFN_TM,),
        in_specs=[
            pl.BlockSpec((FFN_TM, D_MODEL), lambda i: (i, 0)),
            pl.BlockSpec((1, D_MODEL), const),
            pl.BlockSpec(wg.shape, const, pipeline_mode=once),
            pl.BlockSpec(wu.shape, const, pipeline_mode=once),
            pl.BlockSpec(wd.shape, const, pipeline_mode=once),
            pl.BlockSpec((1, D_MODEL), const),
        ],
        out_specs=pl.BlockSpec((FFN_TM, D_MODEL), lambda i: (i, 0)),
        out_shape=jax.ShapeDtypeStruct((m, D_MODEL), jnp.float32),
        compiler_params=pltpu.CompilerParams(
            dimension_semantics=("arbitrary",),
            vmem_limit_bytes=VMEM_LIMIT),
        name="ffn",
    )(x1, nw1, wg, wu, wd, nw2)


def kernel(x, pre_mix_w, w_in, b_gate, lambda_q1, lambda_k1, lambda_q2, lambda_k2, subln_w, rpb,
           w_branch_a, w_branch_b, w_out, post_mix_w, pre_ffn_w, w_gate, w_up, w_down, post_ffn_w):
    batch, seq, _ = x.shape
    depth = w_in.shape[0]
    assert depth == 1, "LAMBDA_INIT is specialised to a single layer"
    bf = jnp.bfloat16
    f32 = jnp.float32
    slopes = np.asarray([2.0 ** (-8.0 * (i + 1) / DA_HEADS) for i in range(DA_HEADS)], np.float32)
    grp = (jnp.arange(DA_W)[:, None] // HEAD_DIM == jnp.arange(LANES)[None, :]).astype(bf)
    x2 = x.reshape(batch * seq, D_MODEL)
    for l in range(depth):
        proj_t, proj_n, stats = _in_proj(x2, pre_mix_w[l].reshape(1, D_MODEL).astype(f32), w_in[l].astype(f32),
                                         b_gate[l].reshape(1, 2 * D_MODEL).astype(f32), grp)
        lam_vecs = jnp.stack([lambda_q1[l], lambda_k1[l], lambda_q2[l], lambda_k2[l]]).astype(f32)
        ya = _diff_attn(proj_t, proj_n, stats, slopes, lam_vecs,
                        subln_w[l].reshape(2 * HEAD_DIM, 1).astype(f32), batch, seq)
        yb = _nbr_attn(proj_n, stats, rpb[l], batch, seq)
        x1 = _mix_out(x2, ya, yb, proj_n, w_branch_a[l].astype(f32), w_branch_b[l].astype(f32),
                      w_out[l].astype(f32), post_mix_w[l].reshape(1, D_MODEL).astype(f32))
        x2 = _ffn(x1, pre_ffn_w[l].reshape(1, D_MODEL).astype(f32), w_gate[l].astype(bf),
                  w_up[l].astype(bf), w_down[l].astype(bf), post_ffn_w[l].reshape(1, D_MODEL).astype(f32))
    return x2.reshape(batch, seq, D_MODEL)
```

```python
import functools
import math

import jax
import jax.numpy as jnp
import numpy as np
from jax import lax
from jax.experimental import pallas as pl
from jax.experimental.pallas import tpu as pltpu

D_MODEL = 1024
HEAD_DIM = 64
DA_HEADS = 4
DA_W = DA_HEADS * 2 * HEAD_DIM
NA_HEADS = 8
NA_PAIRS = NA_HEADS // 2
NA_W = NA_HEADS * HEAD_DIM
GRID_W = 64
NA_ROWS = 8
NA_COLS = 16
D_FF = 2816
IN_COLS = 5120
EPS = 1e-6
NEG = -1e30
LANES = 128
LOG2E = math.log2(math.e)
LAMBDA_INIT = 0.8 - 0.6 * math.exp(-0.3 * 0)

W_QA, W_KA, W_VA = 0, DA_W, 2 * DA_W
W_QB, W_KB = 3 * DA_W, 3 * DA_W + NA_W
W_GATES = 3 * DA_W + 3 * NA_W
CB_KA, CB_QB, CB_KB, CB_VB, CB_GA, CB_GB = 0, 4, 8, 12, 16, 24
N_COLBLK_N = 32
RB_QA, RB_VA = 0, 4

VMEM_LIMIT = 56 * 1024 * 1024


def _rms(xf, w):
    return xf * lax.rsqrt(jnp.mean(xf * xf, axis=-1, keepdims=True) + EPS) * w


IN_TM = 512
IN_TN = 1024
IN_NBLK = 2 * DA_W // LANES
Q_SCALE = HEAD_DIM ** -0.5 * LOG2E
STAT_ROWS = 8


def _max_group_sqnorm(a, grp):
    ab = a.astype(jnp.bfloat16).astype(jnp.float32)
    n2 = jnp.dot((ab * ab).astype(jnp.bfloat16), grp, preferred_element_type=jnp.float32)
    return jnp.max(n2, axis=0, keepdims=True)


def _in_proj_kernel(x_ref, nw_ref, w_ref, b_ref, grp_ref, ot_ref, on_ref, st_ref, wb_ref):
    @pl.when(pl.program_id(0) == 0)
    def _():
        for c in range(IN_COLS // IN_TN):
            wb_ref[:, c * IN_TN:(c + 1) * IN_TN] = w_ref[:, c * IN_TN:(c + 1) * IN_TN].astype(wb_ref.dtype)

    h = _rms(x_ref[...], nw_ref[...]).astype(jnp.bfloat16)

    def proj(col0, ncols):
        return jnp.dot(h, wb_ref[:, col0:col0 + ncols], preferred_element_type=jnp.float32)

    def store_natural(blk0, val):
        for j in range(val.shape[1] // LANES):
            on_ref[blk0 + j] = val[:, j * LANES:(j + 1) * LANES].astype(on_ref.dtype)

    def store_transposed(blk0, val):
        for j in range(val.shape[1] // LANES):
            ot_ref[blk0 + j, 0] = val[:, j * LANES:(j + 1) * LANES].T.astype(ot_ref.dtype)

    grp = grp_ref[...]
    qa = proj(W_QA, DA_W) * Q_SCALE
    store_transposed(RB_QA, qa)
    q_stat = _max_group_sqnorm(qa, grp)
    store_transposed(RB_VA, proj(W_VA, DA_W))
    ka = proj(W_KA, DA_W)
    store_natural(CB_KA, ka)
    k_stat = _max_group_sqnorm(ka, grp)
    qb = proj(W_QB, NA_W) * Q_SCALE
    store_natural(CB_QB, qb)
    qb_stat = _max_group_sqnorm(qb, grp)
    kv = proj(W_KB, 2 * NA_W)
    store_natural(CB_KB, kv)
    kb_stat = _max_group_sqnorm(kv[:, :NA_W], grp)
    for half in range(2):
        cols = slice(half * D_MODEL, (half + 1) * D_MODEL)
        store_natural(CB_GA + half * (D_MODEL // LANES),
                      jax.nn.sigmoid(proj(W_GATES + half * D_MODEL, D_MODEL) + b_ref[:, cols]))
    st_ref[0] = jnp.concatenate(
        [k_stat, q_stat, kb_stat, qb_stat, jnp.zeros((STAT_ROWS - 4, LANES), jnp.float32)], axis=0)


def _in_proj(x2, nw, w_f32, b_gate_row, grp):
    m = x2.shape[0]
    nt = m // IN_TM
    const = lambda i: (0, 0)
    return pl.pallas_call(
        _in_proj_kernel,
        grid=(nt,),
        in_specs=[
            pl.BlockSpec((IN_TM, D_MODEL), lambda i: (i, 0)),
            pl.BlockSpec((1, D_MODEL), const),
            pl.BlockSpec((D_MODEL, IN_COLS), const, pipeline_mode=pl.Buffered(1)),
            pl.BlockSpec((1, 2 * D_MODEL), const),
            pl.BlockSpec((DA_W, LANES), const),
        ],
        out_specs=[
            pl.BlockSpec((IN_NBLK, 1, LANES, IN_TM), lambda i: (0, i, 0, 0)),
            pl.BlockSpec((N_COLBLK_N, IN_TM, LANES), lambda i: (0, i, 0)),
            pl.BlockSpec((1, STAT_ROWS, LANES), lambda i: (i, 0, 0)),
        ],
        out_shape=[
            jax.ShapeDtypeStruct((IN_NBLK, nt, LANES, IN_TM), jnp.bfloat16),
            jax.ShapeDtypeStruct((N_COLBLK_N, m, LANES), jnp.bfloat16),
            jax.ShapeDtypeStruct((nt, STAT_ROWS, LANES), jnp.float32),
        ],
        scratch_shapes=[pltpu.VMEM((D_MODEL, IN_COLS), jnp.bfloat16)],
        compiler_params=pltpu.CompilerParams(
            dimension_semantics=("arbitrary",),
            vmem_limit_bytes=VMEM_LIMIT),
        name="in_proj",
    )(x2, nw, w_f32, b_gate_row, grp)


DA_TQ = 1024
DA_TK = 512
DA_QG = 256
DA_NG = DA_TQ // DA_QG
DA_QT = DA_TQ // IN_TM
DA_DIAG = DA_TQ // DA_TK
DA_NAUG = 9
BF16_EXACT_INT = 256
DA_AHEAD = 4
DA_SKIP_LOG2 = 127.0
DA_FIXED_REACH = 64.0
DA_NBOUND = 5
DA_NEAR_LEFT = ((1, 0), (1, 1))
DA_NEAR_RIGHT = ((0, 3), (0, 2))
DA_NORM_MARGIN = 1.01


def _alibi_tables(slopes):
    f32, bf = np.float32, jnp.bfloat16
    s2 = slopes.astype(f32) * f32(LOG2E)
    c1 = s2.astype(bf)
    c2 = (s2 - c1.astype(f32)).astype(bf)
    c3 = (s2 - c1.astype(f32) - c2.astype(f32)).astype(bf)
    parts = np.stack([c1, c2, c3], axis=1).astype(f32)
    s2_used = parts[:, 0] + parts[:, 1] + parts[:, 2]
    dk = np.arange(DA_TK)
    dk_lo = (dk % BF16_EXACT_INT).astype(f32)
    dk_hi = (dk - dk % BF16_EXACT_INT).astype(f32)
    ones3 = np.ones((3,), f32)
    k_aug = np.concatenate([
        np.broadcast_to((dk_lo[:, None] * ones3)[None], (DA_HEADS, DA_TK, 3)),
        np.broadcast_to((dk_hi[:, None] * ones3)[None], (DA_HEADS, DA_TK, 3)),
        np.broadcast_to(parts[:, None, :], (DA_HEADS, DA_TK, 3)),
    ], axis=-1)
    pad_k = np.zeros((DA_HEADS, DA_TK, HEAD_DIM - DA_NAUG), f32)
    k_aug = np.concatenate([k_aug, pad_k], axis=-1)
    zeros_k = np.zeros((DA_HEADS, DA_TK, HEAD_DIM), f32)
    k_tab = np.stack([np.concatenate([zeros_k, k_aug], axis=-1),
                      np.concatenate([k_aug, zeros_k], axis=-1)], axis=1)
    dq = (np.arange(DA_TQ) % DA_QG).astype(f32)
    sig = np.asarray([1.0, -1.0], f32)
    q_par = sig[None, :, None, None] * parts[:, None, :, None] * np.ones((DA_TQ,), f32)
    q_off = -sig[None, :, None, None] * np.broadcast_to(dq, (DA_HEADS, 1, 3, DA_TQ))
    q_aug = np.concatenate([q_par, q_par, q_off], axis=2)
    q_aug = np.concatenate(
        [q_aug, np.zeros((DA_HEADS, 2, HEAD_DIM - DA_NAUG, DA_TQ), f32)], axis=2)
    return k_tab.astype(bf), q_aug.astype(bf), s2_used


def _diff_attn_kernel(s2_ref, bounds_ref, refpt_ref, q_ref, k_ref, v_ref, ktab_ref, qaug_ref, lam_ref, sw_ref, o_ref,
                      ka_ref, qm_ref, dist_ref, m_ref, l_ref, acc_ref, *, nkc):
    h = pl.program_id(1)
    qi = pl.program_id(2)
    slope2 = s2_ref[h]

    @pl.when(qi == 0)
    def _():
        lane = lax.broadcasted_iota(jnp.int32, (DA_TK, LANES), 1)
        for t in range(nkc):
            rows = slice(t * DA_TK, (t + 1) * DA_TK)
            kt = k_ref[rows, :]
            ka_ref[0, rows, :] = jnp.where(lane < HEAD_DIM, kt, ktab_ref[0])
            ka_ref[1, rows, :] = jnp.where(lane >= HEAD_DIM, kt, ktab_ref[1])
        dq = lax.broadcasted_iota(jnp.int32, (DA_TK, DA_QG), 1)
        dk = lax.broadcasted_iota(jnp.int32, (DA_TK, DA_QG), 0)
        dist_ref[...] = (dq - dk).astype(jnp.float32) * slope2

    for t in range(DA_QT):
        qt = q_ref[t]
        cols = slice(t * IN_TM, (t + 1) * IN_TM)
        for sg in range(2):
            qm_ref[sg, 0, :HEAD_DIM, cols] = qt[:HEAD_DIM]
            qm_ref[sg, 0, HEAD_DIM:, cols] = qaug_ref[sg, :, cols]
            qm_ref[sg, 1, :HEAD_DIM, cols] = qaug_ref[sg, :, cols]
            qm_ref[sg, 1, HEAD_DIM:, cols] = qt[HEAD_DIM:]

    step = (pl.program_id(0) * DA_HEADS + h) * pl.num_programs(2) + qi

    def make_body(sg, diagonal, fixed_ref=False, positions=None, nblocks=1):
        tiles = [(sub, g, mp) for sub in range(DA_DIAG * nblocks) for g in range(DA_NG) for mp in range(2)
                 if positions is None or (sub, g) in positions]

        def side(sub, g):
            if not diagonal:
                return sg, (1.0 if sg == 0 else -1.0), False
            c = g * DA_QG - sub * DA_TK
            if c >= DA_TK:
                return 0, 1.0, False
            if c <= -DA_QG:
                return 1, -1.0, False
            return 0, 1.0, True

        def body(kb, carry):
            def origin(sub, g):
                kc = kb * DA_DIAG + sub
                return (qi * DA_TQ + g * DA_QG - kc * DA_TK).astype(jnp.float32) * slope2

            corrections = {}

            def correction(sub, g):
                if (sub, g) not in corrections:
                    corrections[sub, g] = 2.0 * jnp.minimum(dist_ref[...] + origin(sub, g), 0.0)
                return corrections[sub, g]

            def scores(sub, g, mp):
                cols = slice(g * DA_QG, (g + 1) * DA_QG)
                k0 = pl.multiple_of((kb * DA_DIAG + sub) * DA_TK, DA_TK)
                variant, _, straddles = side(sub, g)
                st = jnp.dot(ka_ref[mp, pl.ds(k0, DA_TK), :], qm_ref[variant, mp, :, cols],
                             preferred_element_type=jnp.float32)
                return st + correction(sub, g) if straddles else st

            ahead = [scores(*tiles[i]) for i in range(min(DA_AHEAD, len(tiles)))]
            for ti, (sub, g, mp) in enumerate(tiles):
                cols = slice(g * DA_QG, (g + 1) * DA_QG)
                st = ahead.pop(0)
                if ti + DA_AHEAD < len(tiles):
                    ahead.append(scores(*tiles[ti + DA_AHEAD]))
                shift = side(sub, g)[1] * origin(sub, g)
                vt = v_ref[kb * DA_DIAG + sub]
                if fixed_ref:
                    p = jnp.exp2(st - (refpt_ref[step * 2 + mp] + shift))
                    p_sum = jnp.sum(p, axis=0, keepdims=True)
                    pv = jnp.dot(vt, p.astype(jnp.bfloat16), preferred_element_type=jnp.float32)
                    if diagonal and sub == 0:
                        l_ref[mp, :, cols] = p_sum
                        acc_ref[mp, :, cols] = pv
                    else:
                        l_ref[mp, :, cols] = l_ref[mp, :, cols] + p_sum
                        acc_ref[mp, :, cols] = acc_ref[mp, :, cols] + pv
                else:
                    m_old = m_ref[mp, :, cols]
                    m_new = jnp.maximum(m_old, jnp.max(st, axis=0, keepdims=True) - shift)
                    alpha = jnp.exp2(m_old - m_new)
                    p = jnp.exp2(st - (m_new + shift))
                    l_ref[mp, :, cols] = alpha * l_ref[mp, :, cols] + jnp.sum(p, axis=0, keepdims=True)
                    acc_ref[mp, :, cols] = alpha * acc_ref[mp, :, cols] + jnp.dot(
                        vt, p.astype(jnp.bfloat16), preferred_element_type=jnp.float32)
                    m_ref[mp, :, cols] = m_new
            return carry

        return body

    lo = bounds_ref[step * DA_NBOUND]
    hi = bounds_ref[step * DA_NBOUND + 1]
    fixed_ok = bounds_ref[step * DA_NBOUND + 2]
    near_l = bounds_ref[step * DA_NBOUND + 3]
    near_r = bounds_ref[step * DA_NBOUND + 4]

    @pl.when(fixed_ok == 1)
    def _():
        make_body(0, True, True)(qi, 0)
        for sg, first, count in ((0, lo, qi - lo), (1, qi + 1, hi - qi - 1)):
            odd = count % 2
            single = make_body(sg, False, True)
            pair = make_body(sg, False, True, nblocks=2)
            lax.fori_loop(first, first + odd, single, 0)
            lax.fori_loop(0, count // 2, lambda j, c, pair=pair, start=first + odd: pair(start + 2 * j, c), 0)

        @pl.when(near_l == 1)
        def _():
            make_body(0, False, True, DA_NEAR_LEFT)(lo - 1, 0)

        @pl.when(near_r == 1)
        def _():
            make_body(1, False, True, DA_NEAR_RIGHT)(hi, 0)

    @pl.when(fixed_ok != 1)
    def _():
        m_ref[...] = jnp.full_like(m_ref, -jnp.inf)
        l_ref[...] = jnp.zeros_like(l_ref)
        acc_ref[...] = jnp.zeros_like(acc_ref)
        make_body(0, True)(qi, 0)
        lax.fori_loop(lo - near_l, qi, make_body(0, False), 0)
        lax.fori_loop(qi + 1, hi + near_r, make_body(1, False), 0)

    lv = lam_ref[...]
    lam = (jnp.exp(jnp.sum(lv[0:1] * lv[1:2], axis=-1, keepdims=True))
           - jnp.exp(jnp.sum(lv[2:3] * lv[3:4], axis=-1, keepdims=True))
           + LAMBDA_INIT)
    o = acc_ref[0] / l_ref[0] - lam * (acc_ref[1] / l_ref[1])
    o = o * lax.rsqrt(jnp.mean(o * o, axis=0, keepdims=True) + EPS) * sw_ref[...] * (1.0 - LAMBDA_INIT)
    o_ref[...] = o.T.astype(o_ref.dtype)


def _skip_bounds(stats, s2_used, batch, seq):
    nq, nkc = seq // DA_TQ, seq // DA_TK
    nkb = nkc // DA_DIAG
    norms = DA_NORM_MARGIN * jnp.sqrt(stats[:, :2, :2 * DA_HEADS])
    norms = norms.reshape(batch, nkc, 2, DA_HEADS, 2)
    kn = norms[:, :, 0]
    qn = norms[:, :, 1].reshape(batch, nq, DA_QT, DA_HEADS, 2).max(axis=2)
    kn_self = kn.reshape(batch, nq, DA_DIAG, DA_HEADS, 2).max(axis=2)
    reach = qn[:, :, None] * (kn[:, None, :] + kn_self[:, :, None])
    q_lo = (np.arange(nq)[:, None] * DA_TQ + np.arange(DA_NG)[None, :] * DA_QG)[:, :, None]
    k_lo = (np.arange(nkc) * DA_TK)[None, None, :]
    gap = np.maximum(np.maximum(q_lo - (k_lo + DA_TK - 1), k_lo - (q_lo + DA_QG - 1)), 0)
    bound = (reach[:, :, None] - s2_used[None, None, None, None, :, None]
             * gap[None, :, :, :, None, None].astype(np.float32))
    need = jnp.any(~(bound < -DA_SKIP_LOG2), axis=-1) | (gap == 0)[None, :, :, :, None]
    need = need.reshape(batch, nq, DA_NG, nkb, DA_DIAG, DA_HEADS)
    near = np.zeros((2, DA_NG, DA_DIAG), bool)
    for side, positions in enumerate((DA_NEAR_LEFT, DA_NEAR_RIGHT)):
        for sub, g in positions:
            near[side, g, sub] = True
    blk = np.arange(nkb)
    left = (blk[None, :] < np.arange(nq)[:, None])[:, None, :, None]
    right = (blk[None, :] > np.arange(nq)[:, None])[:, None, :, None]
    in_near = (left & near[0][None, :, None, :]) | (right & near[1][None, :, None, :])
    need_any = need.any(axis=(2, 4))
    need_far = (need & ~in_near[None, :, :, :, :, None]).any(axis=(2, 4))
    blk = blk[None, None, :, None]
    lo_far = jnp.min(jnp.where(need_far, blk, nkb), axis=2)
    lo_any = jnp.min(jnp.where(need_any, blk, nkb), axis=2)
    hi_far = jnp.max(jnp.where(need_far, blk + 1, 0), axis=2)
    hi_any = jnp.max(jnp.where(need_any, blk + 1, 0), axis=2)
    near_l = lo_any == lo_far - 1
    near_r = hi_any == hi_far + 1
    lo = jnp.where(lo_any >= lo_far - 1, lo_far, lo_any)
    hi = jnp.where(hi_any <= hi_far + 1, hi_far, hi_any)
    refpt = -(qn * kn_self)
    reach_blk = reach.reshape(batch, nq, nkb, DA_DIAG, DA_HEADS, 2).max(axis=(3, 5))
    visited = (blk >= jnp.minimum(lo, lo_any)[:, :, None]) & (blk < jnp.maximum(hi, hi_any)[:, :, None])
    fixed_ok = jnp.all(~visited | (reach_blk <= DA_FIXED_REACH), axis=2)
    bounds = jnp.stack([lo, hi, fixed_ok.astype(lo.dtype), near_l.astype(lo.dtype),
                        near_r.astype(lo.dtype)], axis=-1)
    return (jnp.transpose(bounds, (0, 2, 1, 3)).reshape(-1).astype(jnp.int32),
            jnp.transpose(refpt, (0, 2, 1, 3)).reshape(-1).astype(jnp.float32))


def _diff_attn(proj_t, proj_n, stats, slopes, lam_vecs, subln_col, batch, seq):
    nq = seq // DA_TQ
    nkc = seq // DA_TK
    m = batch * seq
    k_tab, q_aug, s2_used = _alibi_tables(slopes)
    bounds, refpt = _skip_bounds(stats, s2_used, batch, seq)
    grid_spec = pltpu.PrefetchScalarGridSpec(
        num_scalar_prefetch=3,
        grid=(batch, DA_HEADS, nq),
        in_specs=[
            pl.BlockSpec((None, DA_QT, LANES, IN_TM), lambda b, h, qi, s, bd, rp: (RB_QA + h, b * nq + qi, 0, 0)),
            pl.BlockSpec((None, seq, LANES), lambda b, h, qi, s, bd, rp: (CB_KA + h, b, 0)),
            pl.BlockSpec((None, nkc, LANES, DA_TK), lambda b, h, qi, s, bd, rp: (RB_VA + h, b, 0, 0)),
            pl.BlockSpec((None, 2, DA_TK, LANES), lambda b, h, qi, s, bd, rp: (h, 0, 0, 0)),
            pl.BlockSpec((None, 2, HEAD_DIM, DA_TQ), lambda b, h, qi, s, bd, rp: (h, 0, 0, 0)),
            pl.BlockSpec((4, HEAD_DIM), lambda b, h, qi, s, bd, rp: (0, 0)),
            pl.BlockSpec((2 * HEAD_DIM, 1), lambda b, h, qi, s, bd, rp: (0, 0)),
        ],
        out_specs=pl.BlockSpec((None, DA_TQ, LANES), lambda b, h, qi, s, bd, rp: (h, b * nq + qi, 0)),
        scratch_shapes=[
            pltpu.VMEM((2, seq, LANES), jnp.bfloat16),
            pltpu.VMEM((2, 2, LANES, DA_TQ), jnp.bfloat16),
            pltpu.VMEM((DA_TK, DA_QG), jnp.float32),
            pltpu.VMEM((2, 1, DA_TQ), jnp.float32),
            pltpu.VMEM((2, 1, DA_TQ), jnp.float32),
            pltpu.VMEM((2, LANES, DA_TQ), jnp.float32),
        ],
    )
    return pl.pallas_call(
        functools.partial(_diff_attn_kernel, nkc=nkc),
        grid_spec=grid_spec,
        out_shape=jax.ShapeDtypeStruct((DA_HEADS, m, LANES), jnp.bfloat16),
        compiler_params=pltpu.CompilerParams(
            dimension_semantics=("arbitrary", "arbitrary", "arbitrary"),
            vmem_limit_bytes=VMEM_LIMIT),
        name="diff_attn",
    )(s2_used, bounds, refpt, proj_t, proj_n, proj_t, k_tab, q_aug, lam_vecs, subln_col)


NA_GROUP = 64
NA_TOK = NA_GROUP * GRID_W
NA_WIN = NA_ROWS * GRID_W
NA_FIXED_REACH = 64.0
NA_AHEAD = 3
NA_BIAS_ROWS = (2 * NA_ROWS - 1) * GRID_W


def _nbr_attn_kernel(ok_ref, refpt_ref, q_ref, k_ref, v_ref, bias_ref, o_ref, *, rows):
    g = pl.program_id(2)
    step = (pl.program_id(0) * NA_PAIRS + pl.program_id(1)) * pl.num_programs(2) + g
    lane_q = lax.broadcasted_iota(jnp.int32, (GRID_W, LANES), 1)
    dn_last = (((1,), (1,)), ((), ()))
    dn_first = (((0,), (0,)), ((), ()))
    ones = jnp.ones((NA_WIN, LANES), jnp.bfloat16)

    def key_start(rr):
        r = g * NA_GROUP + rr
        rs = jnp.clip(r - NA_ROWS // 2, 0, rows - NA_ROWS)
        return r, rs, pl.multiple_of(rs * GRID_W, GRID_W)

    def scores(rr):
        r, rs, k0 = key_start(rr)
        b0 = pl.multiple_of((NA_ROWS - 1 - (r - rs)) * GRID_W, GRID_W)
        q = q_ref[rr * GRID_W:(rr + 1) * GRID_W, :]
        zero = jnp.zeros_like(q)
        qbd = jnp.concatenate([jnp.where(lane_q < HEAD_DIM, q, zero),
                               jnp.where(lane_q >= HEAD_DIM, q, zero)], axis=0)
        return lax.dot_general(k_ref[pl.ds(k0, NA_WIN), :], qbd, dn_last,
                               preferred_element_type=jnp.float32) + bias_ref[pl.ds(b0, NA_WIN), :]

    def group(fixed_ref):
        if fixed_ref:
            lane_k = lax.broadcasted_iota(jnp.int32, (1, LANES), 1)
            ref_row = jnp.where(lane_k < HEAD_DIM, refpt_ref[step * 2], refpt_ref[step * 2 + 1])
        ahead = [scores(rr) for rr in range(NA_AHEAD)]
        for rr in range(NA_GROUP):
            st = ahead.pop(0)
            if rr + NA_AHEAD < NA_GROUP:
                ahead.append(scores(rr + NA_AHEAD))
            p = jnp.exp2(st - (ref_row if fixed_ref else jnp.max(st, axis=0, keepdims=True)))
            v_ext = jnp.concatenate([v_ref[pl.ds(key_start(rr)[2], NA_WIN), :], ones], axis=1)
            full = lax.dot_general(p.astype(jnp.bfloat16), v_ext, dn_first,
                                   preferred_element_type=jnp.float32)
            num = jnp.where(lane_q < HEAD_DIM, full[:GRID_W, :LANES], full[GRID_W:, :LANES])
            den = jnp.where(lane_q < HEAD_DIM, full[:GRID_W, LANES:], full[GRID_W:, LANES:])
            o_ref[rr * GRID_W:(rr + 1) * GRID_W, :] = (num / den).astype(o_ref.dtype)

    for fixed in (True, False):
        pl.when((ok_ref[step] == 1) if fixed else (ok_ref[step] != 1))(functools.partial(group, fixed))


def _nbr_ref_points(stats, rpb_l, batch, seq):
    ng = seq // NA_TOK
    nt = seq // IN_TM
    assert NA_ROWS * GRID_W <= IN_TM, "a query's window must stay inside its own and the adjacent tiles"
    norms = DA_NORM_MARGIN * jnp.sqrt(stats[:, 2:4, :NA_HEADS]).reshape(batch, nt, 2, NA_HEADS)
    kn, qn = norms[:, :, 0], norms[:, :, 1]
    kn_prev = jnp.concatenate([kn[:, :1], kn[:, :-1]], axis=1)
    kn_next = jnp.concatenate([kn[:, 1:], kn[:, -1:]], axis=1)
    kn_win = jnp.maximum(jnp.maximum(kn_prev, kn), kn_next)
    kn_win = kn_win.reshape(batch, ng, nt // ng, NA_HEADS).max(axis=2)
    qn = qn.reshape(batch, ng, nt // ng, NA_HEADS).max(axis=2)
    bias = rpb_l.astype(jnp.float32) * LOG2E
    b_self = bias[:, NA_ROWS - 1, NA_COLS - 1]
    b_max = jnp.max(bias, axis=(1, 2))
    refpt = -(qn * kn_win) + b_self
    spread = 2.0 * qn * kn_win + (b_max - b_self)
    ok = jnp.all((spread <= NA_FIXED_REACH).reshape(batch, ng, NA_PAIRS, 2), axis=-1)
    refpt = jnp.transpose(refpt.reshape(batch, ng, NA_PAIRS, 2), (0, 2, 1, 3))
    return (jnp.transpose(ok, (0, 2, 1)).reshape(-1).astype(jnp.int32),
            refpt.reshape(-1).astype(jnp.float32))


def _nbr_attn(proj, stats, rpb_l, batch, seq):
    rows = seq // GRID_W
    ng = seq // NA_TOK
    m = batch * seq
    ok, refpt = _nbr_ref_points(stats, rpb_l, batch, seq)
    grid_spec = pltpu.PrefetchScalarGridSpec(
        num_scalar_prefetch=2,
        grid=(batch, NA_PAIRS, ng),
        in_specs=[
            pl.BlockSpec((None, NA_TOK, LANES), lambda b, hp, g, ok, rp: (CB_QB + hp, b * ng + g, 0)),
            pl.BlockSpec((None, seq, LANES), lambda b, hp, g, ok, rp: (CB_KB + hp, b, 0)),
            pl.BlockSpec((None, seq, LANES), lambda b, hp, g, ok, rp: (CB_VB + hp, b, 0)),
            pl.BlockSpec((None, NA_BIAS_ROWS, LANES), lambda b, hp, g, ok, rp: (hp, 0, 0)),
        ],
        out_specs=pl.BlockSpec((None, NA_TOK, LANES), lambda b, hp, g, ok, rp: (hp, b * ng + g, 0)),
    )
    return pl.pallas_call(
        functools.partial(_nbr_attn_kernel, rows=rows),
        grid_spec=grid_spec,
        out_shape=jax.ShapeDtypeStruct((NA_PAIRS, m, LANES), jnp.bfloat16),
        compiler_params=pltpu.CompilerParams(
            dimension_semantics=("arbitrary", "arbitrary", "arbitrary"),
            vmem_limit_bytes=VMEM_LIMIT),
        name="nbr_attn",
    )(ok, refpt, proj, proj, proj, _nbr_bias_table(rpb_l))


def _nbr_bias_table(rpb_l):
    col = np.arange(GRID_W)
    col_start = np.clip(col - NA_COLS // 2, 0, GRID_W - NA_COLS)
    col_in = (col[None, :] >= col_start[:, None]) & (col[None, :] < col_start[:, None] + NA_COLS)
    col_off = np.clip(col[None, :] - col[:, None] + NA_COLS - 1, 0, 2 * NA_COLS - 2)
    ncol = 2 * NA_COLS - 1
    pick = (col_off.T[None] == np.arange(ncol)[:, None, None]).astype(np.float32)
    pick2 = np.zeros((2, ncol, GRID_W, 2, GRID_W), np.float32)
    for hh in range(2):
        pick2[hh, :, :, hh, :] = pick
    rp = rpb_l.astype(jnp.float32).reshape(NA_PAIRS, 2, 2 * NA_ROWS - 1, ncol)
    t = jnp.einsum('pjic,jckhq->pikhq', rp, pick2, precision=lax.Precision.HIGHEST)
    t = jnp.where(col_in.T[None, None, :, None, :], t * LOG2E, NEG)
    return t.reshape(NA_PAIRS, NA_BIAS_ROWS, LANES)


MIX_TM = 1024


def _cat_blocks(ref):
    return jnp.concatenate([ref[c] for c in range(ref.shape[0])], axis=-1)


def _mix_out_kernel(x_ref, ya_ref, yb_ref, ga_ref, gb_ref, wa_ref, wb_ref, wo_ref, nw_ref, o_ref):
    ya = _cat_blocks(ya_ref)
    yb = _cat_blocks(yb_ref)
    ga = _cat_blocks(ga_ref).astype(jnp.float32)
    gb = _cat_blocks(gb_ref).astype(jnp.float32)
    bf = jnp.bfloat16
    merged = (ga * jnp.dot(ya, wa_ref[...].astype(bf), preferred_element_type=jnp.float32)
              + gb * jnp.dot(yb, wb_ref[...].astype(bf), preferred_element_type=jnp.float32))
    t = jnp.dot(merged.astype(bf), wo_ref[...].astype(bf), preferred_element_type=jnp.float32)
    o_ref[...] = x_ref[...] + _rms(t, nw_ref[...])


def _mix_out(x2, ya, yb, proj, wa, wb, wo, nw):
    m = x2.shape[0]
    const = lambda i: (0, 0)
    gate_blk = D_MODEL // LANES
    return pl.pallas_call(
        _mix_out_kernel,
        grid=(m // MIX_TM,),
        in_specs=[
            pl.BlockSpec((MIX_TM, D_MODEL), lambda i: (i, 0)),
            pl.BlockSpec((DA_HEADS, MIX_TM, LANES), lambda i: (0, i, 0)),
            pl.BlockSpec((NA_PAIRS, MIX_TM, LANES), lambda i: (0, i, 0)),
            pl.BlockSpec((gate_blk, MIX_TM, LANES), lambda i: (CB_GA // gate_blk, i, 0)),
            pl.BlockSpec((gate_blk, MIX_TM, LANES), lambda i: (CB_GB // gate_blk, i, 0)),
            pl.BlockSpec(wa.shape, const),
            pl.BlockSpec(wb.shape, const),
            pl.BlockSpec(wo.shape, const),
            pl.BlockSpec((1, D_MODEL), const),
        ],
        out_specs=pl.BlockSpec((MIX_TM, D_MODEL), lambda i: (i, 0)),
        out_shape=jax.ShapeDtypeStruct((m, D_MODEL), jnp.float32),
        compiler_params=pltpu.CompilerParams(
            dimension_semantics=("arbitrary",),
            vmem_limit_bytes=VMEM_LIMIT),
        name="mix_out",
    )(x2, ya, yb, proj, proj, wa, wb, wo, nw)


FFN_TM = 512
FFN_CHUNK = 256
FFN_NCHUNK = D_FF // FFN_CHUNK


def _ffn_kernel(x_ref, nw1_ref, wg_ref, wu_ref, wd_ref, nw2_ref, o_ref):
    x = x_ref[...]
    h = _rms(x, nw1_ref[...]).astype(jnp.bfloat16)
    f = jnp.zeros((FFN_TM, D_MODEL), jnp.float32)
    for c in range(FFN_NCHUNK):
        sl = slice(c * FFN_CHUNK, (c + 1) * FFN_CHUNK)
        g = jnp.dot(h, wg_ref[:, sl], preferred_element_type=jnp.float32)
        u = jnp.dot(h, wu_ref[:, sl], preferred_element_type=jnp.float32)
        a = (jax.nn.silu(g) * u).astype(jnp.bfloat16)
        f = f + jnp.dot(a, wd_ref[sl, :], preferred_element_type=jnp.float32)
    o_ref[...] = x + _rms(f, nw2_ref[...])


def _ffn(x1, nw1, wg, wu, wd, nw2):
    m = x1.shape[0]
    const = lambda i: (0, 0)
    once = pl.Buffered(1)
    return pl.pallas_call(
        _ffn_kernel,
        grid=(m // FFN_TM,),
        in_specs=[
            pl.BlockSpec((FFN_TM, D_MODEL), lambda i: (i, 0)),
            pl.BlockSpec((1, D_MODEL), const),
            pl.BlockSpec(wg.shape, const, pipeline_mode=once),
            pl.BlockSpec(wu.shape, const, pipeline_mode=once),
            pl.BlockSpec(wd.shape, const, pipeline_mode=once),
            pl.BlockSpec((1, D_MODEL), const),
        ],
        out_specs=pl.BlockSpec((FFN_TM, D_MODEL), lambda i: (i, 0)),
        out_shape=jax.ShapeDtypeStruct((m, D_MODEL), jnp.float32),
        compiler_params=pltpu.CompilerParams(
            dimension_semantics=("arbitrary",),
            vmem_limit_bytes=VMEM_LIMIT),
        name="ffn",
    )(x1, nw1, wg, wu, wd, nw2)


def kernel(x, pre_mix_w, w_in, b_gate, lambda_q1, lambda_k1, lambda_q2, lambda_k2, subln_w, rpb,
           w_branch_a, w_branch_b, w_out, post_mix_w, pre_ffn_w, w_gate, w_up, w_down, post_ffn_w):
    batch, seq, _ = x.shape
    depth = w_in.shape[0]
    assert depth == 1, "LAMBDA_INIT is specialised to a single layer"
    bf = jnp.bfloat16
    f32 = jnp.float32
    slopes = np.asarray([2.0 ** (-8.0 * (i + 1) / DA_HEADS) for i in range(DA_HEADS)], np.float32)
    grp = (jnp.arange(DA_W)[:, None] // HEAD_DIM == jnp.arange(LANES)[None, :]).astype(bf)
    x2 = x.reshape(batch * seq, D_MODEL)
    for l in range(depth):
        proj_t, proj_n, stats = _in_proj(x2, pre_mix_w[l].reshape(1, D_MODEL).astype(f32), w_in[l].astype(f32),
                                         b_gate[l].reshape(1, 2 * D_MODEL).astype(f32), grp)
        lam_vecs = jnp.stack([lambda_q1[l], lambda_k1[l], lambda_q2[l], lambda_k2[l]]).astype(f32)
        ya = _diff_attn(proj_t, proj_n, stats, slopes, lam_vecs,
                        subln_w[l].reshape(2 * HEAD_DIM, 1).astype(f32), batch, seq)
        yb = _nbr_attn(proj_n, stats, rpb[l], batch, seq)
        x1 = _mix_out(x2, ya, yb, proj_n, w_branch_a[l].astype(f32), w_branch_b[l].astype(f32),
                      w_out[l].astype(f32), post_mix_w[l].reshape(1, D_MODEL).astype(f32))
        x2 = _ffn(x1, pre_ffn_w[l].reshape(1, D_MODEL).astype(f32), w_gate[l].astype(bf),
                  w_up[l].astype(bf), w_down[l].astype(bf), post_ffn_w[l].reshape(1, D_MODEL).astype(f32))
    return x2.reshape(batch, seq, D_MODEL)
```
